```python
import math
import jax, jax.numpy as jnp
from jax import lax
import numpy as np

D_MODEL = 1024
BATCH = 16
SEQ = 4096
DEPTH = 4

CTX_LEN = 256
GRID_W = 64
HEAD_DIM = 64
ATTN_Q_HEADS = 8
ATTN_KV_HEADS = 2
ATTN_GROUP = ATTN_Q_HEADS // ATTN_KV_HEADS
ATTN_WIDTH = ATTN_Q_HEADS * HEAD_DIM
KV_WIDTH = ATTN_KV_HEADS * HEAD_DIM
CONV_WIDTH = D_MODEL - ATTN_WIDTH
CONV_KERNEL = 31
HYB_IN = ATTN_WIDTH + 2 * KV_WIDTH + 2 * CONV_WIDTH
Q_BLOCK = 128
ROPE_THETA = 10000.0
RWKV_HEADS = D_MODEL // HEAD_DIM
DECAY_LORA = 64
ICLR_LORA = 64
GATE_LORA = 128
D_FF = 2816
FFN_KERNEL = 3
N_EVEN = (DEPTH + 1) // 2
N_ODD = DEPTH // 2
NORM_EPS = 1e-6
LN_EPS = 1e-5
LNX_EPS = 64e-5

kernel_name = "hybrid_gqa_conformer_rwkv7_convffn_dit"


def rmsnorm(x, g, eps=NORM_EPS):
    xf = x.astype(jnp.float32)
    y = xf * lax.rsqrt(jnp.mean(jnp.square(xf), axis=-1, keepdims=True) + eps)
    return (y * g).astype(x.dtype)


def layernorm(x, g, b, eps=LN_EPS):
    xf = x.astype(jnp.float32)
    m = jnp.mean(xf, axis=-1, keepdims=True)
    v = jnp.mean(jnp.square(xf - m), axis=-1, keepdims=True)
    return ((xf - m) * lax.rsqrt(v + eps) * g + b).astype(x.dtype)


def modulate(h, shift, scale):
    return h * (1 + scale) + shift


def depthwise_conv(x, w, b):
    width = w.shape[0]
    out = lax.conv_general_dilated(
        x, w.astype(x.dtype)[:, None, :], window_strides=(1,),
        padding=[(width // 2, width // 2)],
        dimension_numbers=('NWC', 'WIO', 'NWC'),
        feature_group_count=x.shape[-1])
    return out + b


def axial_rope_tables(row, col):
    n_freq = HEAD_DIM // 4
    inv = ROPE_THETA ** (-jnp.arange(n_freq, dtype=jnp.float32) / n_freq)
    ang = jnp.stack([row.astype(jnp.float32)[:, None] * inv,
                     col.astype(jnp.float32)[:, None] * inv], axis=1)
    return jnp.cos(ang), jnp.sin(ang)


def apply_rope(x, cos, sin):
    B, T, H, Dh = x.shape
    xr = x.reshape(B, T, H, 2, 2, Dh // 4)
    x1, x2 = xr[..., 0, :], xr[..., 1, :]
    c = cos[None, :, None]
    s = sin[None, :, None]
    out = jnp.stack([x1 * c - x2 * s, x1 * s + x2 * c], axis=-2)
    return out.reshape(B, T, H, Dh).astype(x.dtype)


def attend(q, k, v):
    s = jnp.einsum('bqhgd,bkhd->bhgqk', q, k) * (HEAD_DIM ** -0.5)
    p = jax.nn.softmax(s, axis=-1)
    return jnp.einsum('bhgqk,bkhd->bqhgd', p, v)


def blocked_attention(q, k, v):
    B, T = q.shape[:2]
    nb = T // Q_BLOCK
    qb = jnp.moveaxis(q.reshape(B, nb, Q_BLOCK, *q.shape[2:]), 1, 0)
    ob = lax.map(lambda qi: attend(qi, k, v), qb)
    return jnp.moveaxis(ob, 0, 1).reshape(B, T, ATTN_WIDTH)


def conformer_conv(u, dw, dw_b, ln_g, ln_b):
    a, g = jnp.split(u, 2, axis=-1)
    h = depthwise_conv(a * jax.nn.sigmoid(g), dw, dw_b)
    return jax.nn.silu(layernorm(h, ln_g, ln_b))


def hybrid_mixer(hc, hl, cos, sin, w_in, q_g, k_g, dw, dw_b, ln_g, ln_b, w_out, ctx_out):
    B, T, _ = hl.shape
    C = hc.shape[1]
    o1, o2, o3 = ATTN_WIDTH, ATTN_WIDTH + KV_WIDTH, ATTN_WIDTH + 2 * KV_WIDTH
    pl = hl @ w_in
    ql = apply_rope(rmsnorm(pl[..., :o1].reshape(B, T, ATTN_Q_HEADS, HEAD_DIM), q_g), cos, sin)
    kl = apply_rope(rmsnorm(pl[..., o1:o2].reshape(B, T, ATTN_KV_HEADS, HEAD_DIM), k_g), cos, sin)
    vl = pl[..., o2:o3].reshape(B, T, ATTN_KV_HEADS, HEAD_DIM)
    if ctx_out:
        pc = hc @ w_in
        kv_c = pc[..., o1:o3]
    else:
        kv_c = hc @ w_in[:, o1:o3]
    kc = rmsnorm(kv_c[..., :KV_WIDTH].reshape(B, C, ATTN_KV_HEADS, HEAD_DIM), k_g)
    vc = kv_c[..., KV_WIDTH:].reshape(B, C, ATTN_KV_HEADS, HEAD_DIM)
    kcf, vcf = kc.astype(jnp.float32), vc.astype(jnp.float32)
    k_all = jnp.concatenate([kcf, kl.astype(jnp.float32)], axis=1)
    v_all = jnp.concatenate([vcf, vl.astype(jnp.float32)], axis=1)
    q5 = ql.astype(jnp.float32).reshape(B, T, ATTN_KV_HEADS, ATTN_GROUP, HEAD_DIM)
    attn_l = blocked_attention(q5, k_all, v_all).astype(hl.dtype)
    conv_l = conformer_conv(pl[..., o3:], dw, dw_b, ln_g, ln_b)
    yl = jnp.concatenate([attn_l, conv_l], axis=-1) @ w_out
    yc = None
    if ctx_out:
        qc = rmsnorm(pc[..., :o1].reshape(B, C, ATTN_Q_HEADS, HEAD_DIM), q_g)
        qc5 = qc.astype(jnp.float32).reshape(B, C, ATTN_KV_HEADS, ATTN_GROUP, HEAD_DIM)
        attn_c = attend(qc5, kcf, vcf).reshape(B, C, ATTN_WIDTH).astype(hc.dtype)
        conv_c = conformer_conv(pc[..., o3:], dw, dw_b, ln_g, ln_b)
        yc = jnp.concatenate([attn_c, conv_c], axis=-1) @ w_out
    return yc, yl


def centred_shift(h):
    p = jnp.pad(h, ((0, 0), (1, 1), (0, 0)))
    return 0.5 * (p[:, :-2] + p[:, 2:]) - h


def wkv_scan(S0, r, decay, k, v, kk, a, reverse):
    xs = tuple(jnp.moveaxis(t.astype(jnp.float32), 1, 0) for t in (r, decay, k, v, kk, a))

    def step(S, inp):
        r_t, w_t, k_t, v_t, kk_t, a_t = inp
        sa = jnp.einsum('bhvk,bhk->bhv', S, kk_t)
        S = (S * w_t[:, :, None, :] - sa[..., None] * (kk_t * a_t)[:, :, None, :]
             + v_t[..., None] * k_t[:, :, None, :])
        return S, jnp.einsum('bhvk,bhk->bhv', S, r_t)

    S, ys = lax.scan(step, S0, xs, reverse=reverse)
    return S, jnp.moveaxis(ys, 0, 1)


def rwkv_mixer(hc, hl, mu, wr, wk, wv, wo, w0, w1, w2, a0, a1, a2, g1, g2, k_k, k_a, u,
               lnx_g, lnx_b, ctx_out):
    B = hl.shape[0]

    def heads(t):
        return t.reshape(t.shape[0], t.shape[1], RWKV_HEADS, HEAD_DIM)

    def features(h):
        xx = centred_shift(h)
        xr, xw, xk, xv, xa, xg = (h + xx * mu[n] for n in range(6))
        k = xk @ wk
        kkf = heads(k * k_k).astype(jnp.float32)
        kk = kkf / jnp.maximum(jnp.sqrt(jnp.sum(jnp.square(kkf), axis=-1, keepdims=True)), 1e-12)
        return {'r': heads(xr @ wr), 'k': k, 'v': heads(xv @ wv),
                'g': jax.nn.sigmoid(xg @ g1) @ g2, 'kk': kk, 'xw': xw, 'xa': xa}

    feats = (features(hc), features(hl))
    ys = ([], [])
    bonus = ([], [])
    for d, rev in enumerate((False, True)):
        S = jnp.zeros((B, RWKV_HEADS, HEAD_DIM, HEAD_DIM), jnp.float32)
        for s, f in enumerate(feats):
            w = -jax.nn.softplus(-(w0[d] + jnp.tanh(f['xw'] @ w1[d]) @ w2[d])) - 0.5
            decay = jnp.exp(-jnp.exp(w.astype(jnp.float32)))
            a = jax.nn.sigmoid(a0[d] + (f['xa'] @ a1[d]) @ a2[d])
            kd = heads(f['k'] * (1 + (a - 1) * k_a))
            S, y = wkv_scan(S, f['r'], heads(decay), kd, f['v'], f['kk'], heads(a), rev)
            if s == 1 or ctx_out:
                ys[s].append(y)
                bonus[s].append(jnp.sum(f['r'] * kd * u[d], axis=-1, keepdims=True) * f['v'])

    def finish(s):
        f = feats[s]
        y = ys[s][0] + ys[s][1]
        m = jnp.mean(y, axis=-1, keepdims=True)
        var = jnp.mean(jnp.square(y - m), axis=-1, keepdims=True)
        y = ((y - m) * lax.rsqrt(var + LNX_EPS)).reshape(y.shape[0], y.shape[1], D_MODEL)
        y = (y * lnx_g + lnx_b).astype(hl.dtype)
        bn = (bonus[s][0] + bonus[s][1]).reshape(y.shape)
        return ((y + bn) * f['g']) @ wo

    yl = finish(1)
    yc = finish(0) if ctx_out else None
    return yc, yl


def conv_ffn(h, w_in, dw, dw_b, w_out):
    gate, val = jnp.split(h @ w_in, 2, axis=-1)
    return (jax.nn.silu(depthwise_conv(gate, dw, dw_b)) * val) @ w_out


def setup_inputs(seed: int = 0) -> dict:
    key = jax.random.key(seed)
    ks = iter(jax.random.split(key, 64))
    f32 = jnp.float32
    D, F = D_MODEL, D_FF

    def nrm(shape, scale):
        return jax.random.normal(next(ks), shape, f32) * scale

    def unif(shape, lo, hi):
        return jax.random.uniform(next(ks), shape, f32, lo, hi)

    return {
        'x': nrm((BATCH, SEQ, D), 1.0),
        'c': nrm((BATCH, D), 1.0),
        'ctx': nrm((BATCH, CTX_LEN, D), 1.0),
        'c_ctx': nrm((D,), 1.0),
        'mod_w': nrm((DEPTH, D, 6 * D), 0.5 * D ** -0.5),
        'mod_b': nrm((DEPTH, 6 * D), 0.02),
        'norm_mix': 1.0 + nrm((DEPTH, D), 0.02),
        'norm_ffn': 1.0 + nrm((DEPTH, D), 0.02),
        'ffn_w_in': nrm((DEPTH, D, 2 * F), D ** -0.5),
        'ffn_dw': nrm((DEPTH, FFN_KERNEL, F), FFN_KERNEL ** -0.5),
        'ffn_dw_b': nrm((DEPTH, F), 0.02),
        'ffn_w_out': nrm((DEPTH, F, D), F ** -0.5),
        'hyb_w_in': nrm((N_EVEN, D, HYB_IN), D ** -0.5),
        'hyb_q_norm': 1.0 + nrm((N_EVEN, HEAD_DIM), 0.02),
        'hyb_k_norm': 1.0 + nrm((N_EVEN, HEAD_DIM), 0.02),
        'hyb_dw': nrm((N_EVEN, CONV_KERNEL, CONV_WIDTH), CONV_KERNEL ** -0.5),
        'hyb_dw_b': nrm((N_EVEN, CONV_WIDTH), 0.02),
        'hyb_ln_g': 1.0 + nrm((N_EVEN, CONV_WIDTH), 0.02),
        'hyb_ln_b': nrm((N_EVEN, CONV_WIDTH), 0.02),
        'hyb_w_out': nrm((N_EVEN, D, D), D ** -0.5),
        'rwkv_mu': unif((N_ODD, 6, D), 0.0, 1.0),
        'rwkv_wr': nrm((N_ODD, D, D), D ** -0.5),
        'rwkv_wk': nrm((N_ODD, D, D), D ** -0.5),
        'rwkv_wv': nrm((N_ODD, D, D), D ** -0.5),
        'rwkv_wo': nrm((N_ODD, D, D), D ** -0.5),
        'rwkv_w0': unif((N_ODD, 2, D), -6.5, -1.5),
        'rwkv_w1': nrm((N_ODD, 2, D, DECAY_LORA), D ** -0.5),
        'rwkv_w2': nrm((N_ODD, 2, DECAY_LORA, D), 0.1 * DECAY_LORA ** -0.5),
        'rwkv_a0': nrm((N_ODD, 2, D), 0.1),
        'rwkv_a1': nrm((N_ODD, 2, D, ICLR_LORA), D ** -0.5),
        'rwkv_a2': nrm((N_ODD, 2, ICLR_LORA, D), 0.5 * ICLR_LORA ** -0.5),
        'rwkv_g1': nrm((N_ODD, D, GATE_LORA), D ** -0.5),
        'rwkv_g2': nrm((N_ODD, GATE_LORA, D), GATE_LORA ** -0.5),
        'rwkv_kk': 0.85 + nrm((N_ODD, D), 0.05),
        'rwkv_ka': 1.0 + nrm((N_ODD, D), 0.05),
        'rwkv_u': nrm((N_ODD, 2, RWKV_HEADS, HEAD_DIM), 0.1),
        'rwkv_lnx_g': 1.0 + nrm((N_ODD, D), 0.02),
        'rwkv_lnx_b': nrm((N_ODD, D), 0.02),
        'final_norm': 1.0 + nrm((D,), 0.02),
    }


def reference(x, c, ctx, c_ctx, mod_w, mod_b, norm_mix, norm_ffn, ffn_w_in, ffn_dw, ffn_dw_b,
              ffn_w_out, hyb_w_in, hyb_q_norm, hyb_k_norm, hyb_dw, hyb_dw_b, hyb_ln_g, hyb_ln_b,
              hyb_w_out, rwkv_mu, rwkv_wr, rwkv_wk, rwkv_wv, rwkv_wo, rwkv_w0, rwkv_w1, rwkv_w2,
              rwkv_a0, rwkv_a1, rwkv_a2, rwkv_g1, rwkv_g2, rwkv_kk, rwkv_ka, rwkv_u, rwkv_lnx_g,
              rwkv_lnx_b, final_norm):
    B, T, D = x.shape
    ROWS = T // GRID_W
    row = jnp.repeat(jnp.arange(ROWS, dtype=jnp.int32), GRID_W)
    col = jnp.tile(jnp.arange(GRID_W, dtype=jnp.int32), ROWS)
    cos, sin = axial_rope_tables(row, col)
    sc, sctx = jax.nn.silu(c), jax.nn.silu(c_ctx)
    xl, xc = x, ctx
    for i in range(DEPTH):
        last = i == DEPTH - 1
        j = i // 2
        mod_l = (sc @ mod_w[i] + mod_b[i]).reshape(B, 6, 1, D)
        mod_c = (sctx @ mod_w[i] + mod_b[i]).reshape(6, D)
        hl = modulate(rmsnorm(xl, norm_mix[i]), mod_l[:, 0], mod_l[:, 1])
        hc = modulate(rmsnorm(xc, norm_mix[i]), mod_c[0], mod_c[1])
        if i % 2 == 0:
            yc, yl = hybrid_mixer(hc, hl, cos, sin, hyb_w_in[j], hyb_q_norm[j], hyb_k_norm[j],
                                  hyb_dw[j], hyb_dw_b[j], hyb_ln_g[j], hyb_ln_b[j], hyb_w_out[j],
                                  not last)
        else:
            yc, yl = rwkv_mixer(hc, hl, rwkv_mu[j], rwkv_wr[j], rwkv_wk[j], rwkv_wv[j], rwkv_wo[j],
                                rwkv_w0[j], rwkv_w1[j], rwkv_w2[j], rwkv_a0[j], rwkv_a1[j],
                                rwkv_a2[j], rwkv_g1[j], rwkv_g2[j], rwkv_kk[j], rwkv_ka[j],
                                rwkv_u[j], rwkv_lnx_g[j], rwkv_lnx_b[j], not last)
        xl = xl + mod_l[:, 2] * yl
        hl = modulate(rmsnorm(xl, norm_ffn[i]), mod_l[:, 3], mod_l[:, 4])
        xl = xl + mod_l[:, 5] * conv_ffn(hl, ffn_w_in[i], ffn_dw[i], ffn_dw_b[i], ffn_w_out[i])
        if not last:
            xc = xc + mod_c[2] * yc
            hc = modulate(rmsnorm(xc, norm_ffn[i]), mod_c[3], mod_c[4])
            xc = xc + mod_c[5] * conv_ffn(hc, ffn_w_in[i], ffn_dw[i], ffn_dw_b[i], ffn_w_out[i])
    return rmsnorm(xl, final_norm)
```

```python
import functools

import jax
import jax.numpy as jnp
from jax import lax
from jax.experimental import pallas as pl
from jax.experimental.pallas import tpu as pltpu

F32 = jnp.float32
BF16 = jnp.bfloat16
HIGHEST = lax.Precision.HIGHEST

HEAD_DIM = 64
ATTN_Q_HEADS = 8
ATTN_KV_HEADS = 2
ATTN_GROUP = ATTN_Q_HEADS // ATTN_KV_HEADS
ATTN_WIDTH = ATTN_Q_HEADS * HEAD_DIM
KV_WIDTH = ATTN_KV_HEADS * HEAD_DIM
CONV_KERNEL = 31
GRID_W = 64
ROPE_THETA = 10000.0
NORM_EPS = 1e-6
LN_EPS = 1e-5
LNX_EPS = 64e-5

V7X_LANES = 128
V7X_SUBLANES = 8
V7X_MXU_DIM = 256
V7X_VMEM_BYTES = 64 * 1024 * 1024
VMEM_LIMIT_BYTES = 58 * 1024 * 1024

HALO = 16
SCAN_CHUNK = 64
SCAN_GROUP = V7X_MXU_DIM // HEAD_DIM


def _cparams(*sem):
    return pltpu.CompilerParams(dimension_semantics=sem, vmem_limit_bytes=VMEM_LIMIT_BYTES)


def _row_tile(t, pref):
    return pref if t % pref == 0 else t


def _dot(a, b, precision=None):
    return jnp.dot(a, b, preferred_element_type=F32, precision=precision)


def _dot_nt(a, b, precision=None):
    return lax.dot_general(a, b, (((1,), (1,)), ((), ())), preferred_element_type=F32, precision=precision)


def _dot_tn(a, b, precision=None):
    return lax.dot_general(a, b, (((0,), (0,)), ((), ())), preferred_element_type=F32, precision=precision)


def _silu(x):
    return x * jax.nn.sigmoid(x)


def _norm_mod(x, g, shift, scale):
    ms = jnp.mean(x * x, axis=-1, keepdims=True)
    return (x * lax.rsqrt(ms + NORM_EPS) * g) * (1.0 + scale) + shift


def _head_sum(x, ones_bd):
    w = x.shape[-1]
    hi = x.astype(BF16)
    lo = (x - hi.astype(F32)).astype(BF16)
    outs = []
    for s in range(0, w, V7X_MXU_DIM):
        e = min(s + V7X_MXU_DIM, w)
        g = ones_bd[: e - s, : e - s]
        outs.append(_dot(hi[:, s:e], g) + _dot(lo[:, s:e], g))
    return outs[0] if len(outs) == 1 else jnp.concatenate(outs, axis=-1)


def _mod_kernel(c_ref, w_ref, b_ref, o_ref):
    s = _silu(c_ref[...])
    o_ref[0] = _dot(s, w_ref[0], HIGHEST) + b_ref[0]


def _modulation(cvec, mod_w, mod_b):
    depth, d, n = mod_w.shape
    m = cvec.shape[0]
    tn = 512
    return pl.pallas_call(
        _mod_kernel,
        grid=(depth, n // tn),
        in_specs=[
            pl.BlockSpec((m, d), lambda l, j: (0, 0)),
            pl.BlockSpec((1, d, tn), lambda l, j: (l, 0, j)),
            pl.BlockSpec((1, 1, tn), lambda l, j: (l, 0, j)),
        ],
        out_specs=pl.BlockSpec((1, m, tn), lambda l, j: (l, 0, j)),
        out_shape=jax.ShapeDtypeStruct((depth, m, n), F32),
        compiler_params=_cparams("parallel", "parallel"),
        name="modulation",
    )(cvec, mod_w, mod_b.reshape(depth, 1, n))


def _nmm_kernel(x_ref, g_ref, sh_ref, sc_ref, w_ref, o_ref, h_ref):
    @pl.when(pl.program_id(2) == 0)
    def _():
        h_ref[...] = _norm_mod(x_ref[0], g_ref[...], sh_ref[0], sc_ref[0]).astype(BF16)

    o_ref[0] = _dot(h_ref[...], w_ref[...]).astype(o_ref.dtype)


def _norm_mod_matmul(x, g, shift, scale, w, tm_pref=512, tn=256, out_dtype=F32):
    b, t, d = x.shape
    n = w.shape[1]
    tm = _row_tile(t, tm_pref)
    return pl.pallas_call(
        _nmm_kernel,
        grid=(b, t // tm, n // tn),
        in_specs=[
            pl.BlockSpec((1, tm, d), lambda bi, i, j: (bi, i, 0)),
            pl.BlockSpec((1, d), lambda bi, i, j: (0, 0)),
            pl.BlockSpec((1, 1, d), lambda bi, i, j: (bi, 0, 0)),
            pl.BlockSpec((1, 1, d), lambda bi, i, j: (bi, 0, 0)),
            pl.BlockSpec((d, tn), lambda bi, i, j: (0, j)),
        ],
        out_specs=pl.BlockSpec((1, tm, tn), lambda bi, i, j: (bi, i, j)),
        out_shape=jax.ShapeDtypeStruct((b, t, n), out_dtype),
        scratch_shapes=[pltpu.VMEM((tm, d), BF16)],
        compiler_params=_cparams("parallel", "parallel", "arbitrary"),
        name="norm_mod_matmul",
    )(x, g.reshape(1, d), shift, scale, w)


def _ffn_kernel(x_ref, xp_ref, xn_ref, g_ref, sh_ref, sc_ref, gt_ref, wg_ref, wv_ref,
                dw_ref, db_ref, wo_ref, o_ref, h_ref, gate_ref, acc_ref, *, tm):
    i = pl.program_id(1)
    j = pl.program_id(2)
    nt = pl.num_programs(1)
    nf = pl.num_programs(2)

    @pl.when(j == 0)
    def _():
        g, sh, sc = g_ref[...], sh_ref[0], sc_ref[0]
        h_ref[0:HALO] = _norm_mod(xp_ref[0], g, sh, sc).astype(BF16)
        h_ref[HALO:HALO + tm] = _norm_mod(x_ref[0], g, sh, sc).astype(BF16)
        h_ref[HALO + tm:2 * HALO + tm] = _norm_mod(xn_ref[0], g, sh, sc).astype(BF16)
        acc_ref[...] = jnp.zeros_like(acc_ref)

    gate_ref[...] = _dot(h_ref[...], wg_ref[...])
    val = _dot(h_ref[HALO:HALO + tm], wv_ref[...])
    rows = lax.broadcasted_iota(jnp.int32, (tm, 1), 0)
    g_prev = jnp.where((rows == 0) & (i == 0), 0.0, gate_ref[pl.ds(HALO - 1, tm), :])
    g_next = jnp.where((rows == tm - 1) & (i == nt - 1), 0.0, gate_ref[pl.ds(HALO + 1, tm), :])
    conv = (g_prev * dw_ref[0:1, :] + gate_ref[pl.ds(HALO, tm), :] * dw_ref[1:2, :]
            + g_next * dw_ref[2:3, :] + db_ref[...])
    act = (_silu(conv) * val).astype(BF16)
    acc_ref[...] += _dot(act, wo_ref[...])

    @pl.when(j == nf - 1)
    def _():
        o_ref[0] = x_ref[0] + gt_ref[0] * acc_ref[...]


def _conv_ffn(x, g, shift, scale, gate, w_in, dw, dw_b, w_out, tm_pref=512):
    b, t, d = x.shape
    f = w_out.shape[0]
    nf = 2
    fc = f // nf
    tm = _row_tile(t, tm_pref)
    hb = tm // HALO
    nhb = t // HALO
    dwp = jnp.zeros((V7X_SUBLANES, f), F32).at[:dw.shape[0]].set(dw)
    return pl.pallas_call(
        functools.partial(_ffn_kernel, tm=tm),
        grid=(b, t // tm, nf),
        in_specs=[
            pl.BlockSpec((1, tm, d), lambda bi, i, j: (bi, i, 0)),
            pl.BlockSpec((1, HALO, d), lambda bi, i, j: (bi, jnp.maximum(i * hb - 1, 0), 0)),
            pl.BlockSpec((1, HALO, d), lambda bi, i, j: (bi, jnp.minimum((i + 1) * hb, nhb - 1), 0)),
            pl.BlockSpec((1, d), lambda bi, i, j: (0, 0)),
            pl.BlockSpec((1, 1, d), lambda bi, i, j: (bi, 0, 0)),
            pl.BlockSpec((1, 1, d), lambda bi, i, j: (bi, 0, 0)),
            pl.BlockSpec((1, 1, d), lambda bi, i, j: (bi, 0, 0)),
            pl.BlockSpec((d, fc), lambda bi, i, j: (0, j)),
            pl.BlockSpec((d, fc), lambda bi, i, j: (0, nf + j)),
            pl.BlockSpec((V7X_SUBLANES, fc), lambda bi, i, j: (0, j)),
            pl.BlockSpec((1, fc), lambda bi, i, j: (0, j)),
            pl.BlockSpec((fc, d), lambda bi, i, j: (j, 0)),
        ],
        out_specs=pl.BlockSpec((1, tm, d), lambda bi, i, j: (bi, i, 0)),
        out_shape=jax.ShapeDtypeStruct((b, t, d), F32),
        scratch_shapes=[
            pltpu.VMEM((tm + 2 * HALO, d), BF16),
            pltpu.VMEM((tm + 2 * HALO, fc), F32),
            pltpu.VMEM((tm, d), F32),
        ],
        compiler_params=_cparams("parallel", "parallel", "arbitrary"),
        name="conv_ffn",
    )(x, x, x, g.reshape(1, d), shift, scale, gate, w_in, w_in, dwp, dw_b.reshape(1, f), w_out)


def _rope(x, cos, sin):
    w = x.shape[-1]
    lane = lax.broadcasted_iota(jnp.int32, x.shape, x.ndim - 1)
    partner = jnp.where(lane % 32 < 16, pltpu.roll(x, w - 16, x.ndim - 1), pltpu.roll(x, 16, x.ndim - 1))
    return x * cos + partner * sin


def _qk_prep_kernel(q_ref, k_ref, v_ref, cos_ref, sin_ref, qg_ref, kg_ref, ones_ref,
                    qo_ref, ko_ref, vo_ref):
    ones_bd = ones_ref[...]
    inv = 1.0 / HEAD_DIM
    q = q_ref[0]
    q = q * lax.rsqrt(_head_sum(q * q, ones_bd) * inv + NORM_EPS) * qg_ref[...]
    q = _rope(q, cos_ref[...], sin_ref[...])
    qo_ref[0] = (q * (HEAD_DIM ** -0.5)).astype(BF16)
    k = k_ref[0]
    k = k * lax.rsqrt(_head_sum(k * k, ones_bd) * inv + NORM_EPS) * kg_ref[...]
    k = _rope(k, cos_ref[:, :KV_WIDTH], sin_ref[:, :KV_WIDTH])
    ko_ref[0] = k.astype(BF16)
    vo_ref[0] = v_ref[0].astype(BF16)


def _qk_prep(proj, cos, sin, q_g, k_g, ones_bd, tm_pref=512):
    b, t, _ = proj.shape
    tm = _row_tile(t, tm_pref)
    kcol = 3 * ATTN_WIDTH // KV_WIDTH
    return pl.pallas_call(
        _qk_prep_kernel,
        grid=(b, t // tm),
        in_specs=[
            pl.BlockSpec((1, tm, ATTN_WIDTH), lambda bi, i: (bi, i, 0)),
            pl.BlockSpec((1, tm, KV_WIDTH), lambda bi, i: (bi, i, kcol)),
            pl.BlockSpec((1, tm, KV_WIDTH), lambda bi, i: (bi, i, kcol + 1)),
            pl.BlockSpec((tm, ATTN_WIDTH), lambda bi, i: (i, 0)),
            pl.BlockSpec((tm, ATTN_WIDTH), lambda bi, i: (i, 0)),
            pl.BlockSpec((1, ATTN_WIDTH), lambda bi, i: (0, 0)),
            pl.BlockSpec((1, KV_WIDTH), lambda bi, i: (0, 0)),
            pl.BlockSpec((V7X_MXU_DIM, V7X_MXU_DIM), lambda bi, i: (0, 0)),
        ],
        out_specs=[
            pl.BlockSpec((1, tm, ATTN_WIDTH), lambda bi, i: (bi, i, 0)),
            pl.BlockSpec((1, tm, KV_WIDTH), lambda bi, i: (bi, i, 0)),
            pl.BlockSpec((1, tm, KV_WIDTH), lambda bi, i: (bi, i, 0)),
        ],
        out_shape=[
            jax.ShapeDtypeStruct((b, t, ATTN_WIDTH), BF16),
            jax.ShapeDtypeStruct((b, t, KV_WIDTH), BF16),
            jax.ShapeDtypeStruct((b, t, KV_WIDTH), BF16),
        ],
        compiler_params=_cparams("parallel", "parallel"),
        name="qk_prep",
    )(proj, proj, proj, cos, sin, jnp.tile(q_g, ATTN_Q_HEADS).reshape(1, ATTN_WIDTH),
      jnp.tile(k_g, ATTN_KV_HEADS).reshape(1, KV_WIDTH), ones_bd)


def _attn_kernel(q_ref, k_ref, v_ref, o_ref):
    for g in range(ATTN_KV_HEADS):
        kg = k_ref[0, :, g * HEAD_DIM:(g + 1) * HEAD_DIM]
        vg = v_ref[0, :, g * HEAD_DIM:(g + 1) * HEAD_DIM]
        for hh in range(ATTN_GROUP):
            lo = (g * ATTN_GROUP + hh) * HEAD_DIM
            s = _dot_nt(q_ref[0, :, lo:lo + HEAD_DIM], kg)
            m = jnp.max(s, axis=-1, keepdims=True)
            p = jnp.exp(s - m)
            l = jnp.sum(p, axis=-1, keepdims=True)
            o = _dot(p.astype(BF16), vg) / l
            o_ref[0, :, lo:lo + HEAD_DIM] = o.astype(o_ref.dtype)


def _attention(q, k, v, tq_pref=256):
    b, t, _ = q.shape
    tk = k.shape[1]
    tq = _row_tile(t, tq_pref)
    return pl.pallas_call(
        _attn_kernel,
        grid=(b, t // tq),
        in_specs=[
            pl.BlockSpec((1, tq, ATTN_WIDTH), lambda bi, i: (bi, i, 0)),
            pl.BlockSpec((1, tk, KV_WIDTH), lambda bi, i: (bi, 0, 0)),
            pl.BlockSpec((1, tk, KV_WIDTH), lambda bi, i: (bi, 0, 0)),
        ],
        out_specs=pl.BlockSpec((1, tq, ATTN_WIDTH), lambda bi, i: (bi, i, 0)),
        out_shape=jax.ShapeDtypeStruct((b, t, ATTN_WIDTH), BF16),
        compiler_params=_cparams("parallel", "parallel"),
        name="attention",
    )(q, k, v)


def _conformer_kernel(a_ref, g_ref, ap_ref, gp_ref, an_ref, gn_ref, dw_ref, db_ref, lg_ref, lb_ref,
                      o_ref, u_ref, *, tm):
    i = pl.program_id(1)
    nt = pl.num_programs(1)
    u_ref[0:HALO] = jnp.where(i == 0, 0.0, ap_ref[0] * jax.nn.sigmoid(gp_ref[0]))
    u_ref[HALO:HALO + tm] = a_ref[0] * jax.nn.sigmoid(g_ref[0])
    u_ref[HALO + tm:2 * HALO + tm] = jnp.where(i == nt - 1, 0.0, an_ref[0] * jax.nn.sigmoid(gn_ref[0]))
    half = CONV_KERNEL // 2
    acc = jnp.zeros((tm, u_ref.shape[1]), F32) + db_ref[...]
    for j in range(CONV_KERNEL):
        acc = acc + u_ref[pl.ds(HALO - half + j, tm), :] * dw_ref[j:j + 1, :]
    mean = jnp.mean(acc, axis=-1, keepdims=True)
    cen = acc - mean
    var = jnp.mean(cen * cen, axis=-1, keepdims=True)
    y = cen * lax.rsqrt(var + LN_EPS) * lg_ref[...] + lb_ref[...]
    o_ref[0] = _silu(y).astype(o_ref.dtype)


def _conformer_conv(proj, dw, dw_b, ln_g, ln_b, tm_pref=256):
    b, t, _ = proj.shape
    cw = dw.shape[1]
    tm = _row_tile(t, tm_pref)
    hb = tm // HALO
    nhb = t // HALO
    dwp = jnp.zeros((32, cw), F32).at[:CONV_KERNEL].set(dw)
    prev = lambda c: (lambda bi, i: (bi, jnp.maximum(i * hb - 1, 0), c))
    nxt = lambda c: (lambda bi, i: (bi, jnp.minimum((i + 1) * hb, nhb - 1), c))
    vec = pl.BlockSpec((1, cw), lambda bi, i: (0, 0))
    return pl.pallas_call(
        functools.partial(_conformer_kernel, tm=tm),
        grid=(b, t // tm),
        in_specs=[
            pl.BlockSpec((1, tm, cw), lambda bi, i: (bi, i, 1)),
            pl.BlockSpec((1, tm, cw), lambda bi, i: (bi, i, 2)),
            pl.BlockSpec((1, HALO, cw), prev(1)),
            pl.BlockSpec((1, HALO, cw), prev(2)),
            pl.BlockSpec((1, HALO, cw), nxt(1)),
            pl.BlockSpec((1, HALO, cw), nxt(2)),
            pl.BlockSpec((32, cw), lambda bi, i: (0, 0)),
            vec, vec, vec,
        ],
        out_specs=pl.BlockSpec((1, tm, cw), lambda bi, i: (bi, i, 0)),
        out_shape=jax.ShapeDtypeStruct((b, t, cw), BF16),
        scratch_shapes=[pltpu.VMEM((tm + 2 * HALO, cw), F32)],
        compiler_params=_cparams("parallel", "parallel"),
        name="conformer_conv",
    )(proj, proj, proj, proj, proj, proj, dwp, dw_b.reshape(1, cw), ln_g.reshape(1, cw), ln_b.reshape(1, cw))


def _hyb_out_kernel(x_ref, a_ref, c_ref, gt_ref, wa_ref, wc_ref, o_ref):
    y = _dot(a_ref[0], wa_ref[...]) + _dot(c_ref[0], wc_ref[...])
    o_ref[0] = x_ref[0] + gt_ref[0] * y


def _hybrid_out(x, attn, conv, gate, w_out, tm_pref=512):
    b, t, d = x.shape
    tm = _row_tile(t, tm_pref)
    aw = attn.shape[-1]
    cw = conv.shape[-1]
    return pl.pallas_call(
        _hyb_out_kernel,
        grid=(b, t // tm),
        in_specs=[
            pl.BlockSpec((1, tm, d), lambda bi, i: (bi, i, 0)),
            pl.BlockSpec((1, tm, aw), lambda bi, i: (bi, i, 0)),
            pl.BlockSpec((1, tm, cw), lambda bi, i: (bi, i, 0)),
            pl.BlockSpec((1, 1, d), lambda bi, i: (bi, 0, 0)),
            pl.BlockSpec((aw, d), lambda bi, i: (0, 0)),
            pl.BlockSpec((cw, d), lambda bi, i: (0, 0)),
        ],
        out_specs=pl.BlockSpec((1, tm, d), lambda bi, i: (bi, i, 0)),
        out_shape=jax.ShapeDtypeStruct((b, t, d), F32),
        compiler_params=_cparams("parallel", "parallel"),
        name="hybrid_out",
    )(x, attn, conv, gate, w_out[:aw], w_out[aw:])


def _rwkv_feat_kernel(x_ref, xp_ref, xn_ref, g_ref, sh_ref, sc_ref, mu_ref, wr_ref, wk_ref, wv_ref,
                      g1_ref, g2_ref, w1_ref, w2f_ref, w2r_ref, a1_ref, a2f_ref, a2r_ref,
                      w0_ref, a0_ref, kk_ref, ka_ref, u_ref, ones_ref,
                      r_o, v_o, kkn_o, gg_o, bonus_o, lwf_o, kdf_o, bf_o, lwr_o, kdr_o, br_o, *, tm):
    i = pl.program_id(1)
    nt = pl.num_programs(1)
    g, sh, sc = g_ref[...], sh_ref[0], sc_ref[0]
    h = _norm_mod(x_ref[0], g, sh, sc)
    hp = jnp.where(i == 0, 0.0, _norm_mod(xp_ref[0, HALO - 1:HALO, :], g, sh, sc))
    hn = jnp.where(i == nt - 1, 0.0, _norm_mod(xn_ref[0, 0:1, :], g, sh, sc))
    rows = lax.broadcasted_iota(jnp.int32, (tm, 1), 0)
    up = jnp.where(rows == 0, hp, pltpu.roll(h, 1, 0))
    dn = jnp.where(rows == tm - 1, hn, pltpu.roll(h, tm - 1, 0))
    xx = 0.5 * (up + dn) - h

    def mix(n):
        return (h + xx * mu_ref[n:n + 1, :]).astype(BF16)

    r = _dot(mix(0), wr_ref[...])
    k = _dot(mix(2), wk_ref[...])
    v = _dot(mix(3), wv_ref[...])
    gg = _dot(jax.nn.sigmoid(_dot(mix(5), g1_ref[...])).astype(BF16), g2_ref[...])
    tl = jnp.tanh(_dot(mix(1), w1_ref[...])).astype(BF16)
    al = _dot(mix(4), a1_ref[...]).astype(BF16)

    ones_bd = ones_ref[...]
    kkf = k * kk_ref[...]
    kkn = kkf / jnp.maximum(jnp.sqrt(_head_sum(kkf * kkf, ones_bd)), 1e-12)
    r_o[0] = r
    v_o[0] = v
    kkn_o[0] = kkn
    gg_o[0] = gg

    bonus = jnp.zeros_like(r)
    outs = ((w2f_ref, a2f_ref, lwf_o, kdf_o, bf_o), (w2r_ref, a2r_ref, lwr_o, kdr_o, br_o))
    for dd, (w2_ref, a2_ref, lw_o, kd_o, b_o) in enumerate(outs):
        z = -(w0_ref[dd:dd + 1, :] + _dot(tl, w2_ref[...]))
        softplus = jnp.maximum(z, 0.0) + jnp.log1p(jnp.exp(-jnp.abs(z)))
        lw_o[0] = -jnp.exp(-softplus - 0.5)
        a = jax.nn.sigmoid(a0_ref[dd:dd + 1, :] + _dot(al, a2_ref[...]))
        kd = k * (1.0 + (a - 1.0) * ka_ref[...])
        kd_o[0] = kd
        b_o[0] = kkn * a
        bonus = bonus + _head_sum(r * kd * u_ref[dd:dd + 1, :], ones_bd) * v
    bonus_o[0] = bonus


def _rwkv_features(x, g, shift, scale, p, ones_bd, tm_pref=256):
    b, t, d = x.shape
    tm = _row_tile(t, tm_pref)
    hb = tm // HALO
    nhb = t // HALO
    row = pl.BlockSpec((1, tm, d), lambda bi, i: (bi, i, 0))
    vec3 = pl.BlockSpec((1, 1, d), lambda bi, i: (bi, 0, 0))

    def full(a):
        return pl.BlockSpec(a.shape, lambda bi, i: (0,) * a.ndim)

    consts = [p["mu"], p["wr"], p["wk"], p["wv"], p["g1"], p["g2"], p["w1"], p["w2f"], p["w2r"],
              p["a1"], p["a2f"], p["a2r"], p["w0"], p["a0"], p["kk"], p["ka"], p["u"], ones_bd]
    n_out = 11
    return pl.pallas_call(
        functools.partial(_rwkv_feat_kernel, tm=tm),
        grid=(b, t // tm),
        in_specs=[
            row,
            pl.BlockSpec((1, HALO, d), lambda bi, i: (bi, jnp.maximum(i * hb - 1, 0), 0)),
            pl.BlockSpec((1, HALO, d), lambda bi, i: (bi, jnp.minimum((i + 1) * hb, nhb - 1), 0)),
            pl.BlockSpec((1, d), lambda bi, i: (0, 0)),
            vec3, vec3,
        ] + [full(a) for a in consts],
        out_specs=[row] * n_out,
        out_shape=[jax.ShapeDtypeStruct((b, t, d), F32)] * n_out,
        compiler_params=_cparams("parallel", "parallel"),
        name="rwkv_features",
    )(x, x, x, g.reshape(1, d), shift, scale, *consts)


def _scan_kernel(r_ref, lw_ref, kd_ref, v_ref, kk_ref, b_ref, s0_ref, ms_ref, mi_ref, bd_ref, eye_ref,
                 y_ref, sf_ref, st_ref, *, reverse, precision):
    c = pl.program_id(1)
    nc = pl.num_programs(1)
    L = r_ref.shape[1]
    W = V7X_MXU_DIM
    ngroups = r_ref.shape[2] // W
    dot = functools.partial(_dot, precision=precision)

    @pl.when(c == 0)
    def _():
        st_ref[...] = s0_ref[0]

    mask_strict = ms_ref[...]
    mask_incl = mi_ref[...]
    bd = bd_ref[...]
    eye = eye_ref[...]
    tri = mask_incl[:, :L]
    eye_row = jnp.concatenate([eye[:L, :L]] * SCAN_GROUP, axis=1)
    last = 0 if reverse else L - 1

    def expand(a):
        return jnp.concatenate([a] * SCAN_GROUP, axis=0) * bd

    for gi in range(ngroups):
        sl = slice(gi * W, (gi + 1) * W)
        lw = lw_ref[0, :, sl]
        cs = dot(tri, lw)
        gam = jnp.exp(cs)
        gam_prev = jnp.exp(cs - lw)
        gam_inv = jnp.exp(-cs)
        gam_end = gam[last:last + 1, :]
        kkg = kk_ref[0, :, sl] * gam_prev
        rg = r_ref[0, :, sl] * gam
        kt = kd_ref[0, :, sl] * gam_inv
        bt = b_ref[0, :, sl] * gam_inv
        v = v_ref[0, :, sl]
        st = st_ref[gi]

        lhs = jnp.concatenate([kkg, rg], axis=0)
        rhs = jnp.concatenate([expand(bt), expand(kt)], axis=0)
        sc = _dot_nt(lhs, rhs, precision)
        m_row = sc[:L, :W] * mask_strict
        n_row = sc[:L, W:] * mask_strict
        ab_row = sc[L:, :W] * mask_incl
        ak_row = sc[L:, W:] * mask_incl

        q = -m_row
        t_row = eye_row + q
        span = 1
        while 2 * span < L:
            q = dot(q, expand(q))
            t_row = t_row + dot(t_row, expand(q))
            span *= 2

        s_prod = dot(lhs, st)
        ev = expand(v)
        nv = dot(jnp.concatenate([n_row, ak_row], axis=0), ev)
        u = dot(t_row, expand(s_prod[:L] + nv[:L]))
        y = s_prod[L:] + nv[L:] - dot(ab_row, expand(u))
        y_ref[0, :, sl] = y

        upd = _dot_tn(jnp.concatenate([kt * gam_end, -(bt * gam_end)], axis=0),
                      jnp.concatenate([v, u], axis=0), precision) * bd
        gcol = jnp.sum(eye * gam_end, axis=1, keepdims=True)
        st_ref[gi] = st * gcol + upd

    @pl.when(c == nc - 1)
    def _():
        sf_ref[0] = st_ref[...]


def _scan_masks(L, reverse):
    t = jnp.arange(L)[:, None]
    i = jnp.arange(L)[None, :]
    strict = (i > t) if reverse else (i < t)
    incl = (i >= t) if reverse else (i <= t)
    tile = lambda m: jnp.tile(m.astype(F32), (1, SCAN_GROUP))
    return tile(strict), tile(incl)


def _wkv_scan(r, lw, kd, v, kk, bvec, s0, reverse, precision=HIGHEST):
    b, t, d = r.shape
    L = SCAN_CHUNK
    nc = t // L
    W = V7X_MXU_DIM
    ng = d // W
    ms, mi = _scan_masks(L, reverse)
    hid = jnp.arange(W) // HEAD_DIM
    bd = (hid[:, None] == hid[None, :]).astype(F32)
    eye = jnp.eye(W, dtype=F32)
    cidx = (lambda bi, c: (bi, nc - 1 - c, 0)) if reverse else (lambda bi, c: (bi, c, 0))
    row = pl.BlockSpec((1, L, d), cidx)
    state = pl.BlockSpec((1, ng, W, W), lambda bi, c: (bi, 0, 0, 0))
    const = lambda a: pl.BlockSpec(a.shape, lambda bi, c: (0, 0))
    return pl.pallas_call(
        functools.partial(_scan_kernel, reverse=reverse, precision=precision),
        grid=(b, nc),
        in_specs=[row] * 6 + [state, const(ms), const(mi), const(bd), const(eye)],
        out_specs=[row, state],
        out_shape=[jax.ShapeDtypeStruct((b, t, d), F32), jax.ShapeDtypeStruct((b, ng, W, W), F32)],
        scratch_shapes=[pltpu.VMEM((ng, W, W), F32)],
        compiler_params=_cparams("parallel", "arbitrary"),
        name="wkv_scan_rev" if reverse else "wkv_scan_fwd",
    )(r, lw, kd, v, kk, bvec, s0, ms, mi, bd, eye)


def _rwkv_out_kernel(x_ref, yf_ref, yr_ref, bonus_ref, gg_ref, gt_ref, lg_ref, lb_ref, wo_ref, ones_ref, o_ref):
    ones_bd = ones_ref[...]
    inv = 1.0 / HEAD_DIM
    y = yf_ref[0] + yr_ref[0]
    cen = y - _head_sum(y, ones_bd) * inv
    var = _head_sum(cen * cen, ones_bd) * inv
    yn = cen * lax.rsqrt(var + LNX_EPS) * lg_ref[...] + lb_ref[...]
    z = ((yn + bonus_ref[0]) * gg_ref[0]).astype(BF16)
    o_ref[0] = x_ref[0] + gt_ref[0] * _dot(z, wo_ref[...])


def _rwkv_out(x, yf, yr, bonus, gg, gate, lnx_g, lnx_b, wo, ones_bd, tm_pref=256):
    b, t, d = x.shape
    tm = _row_tile(t, tm_pref)
    row = pl.BlockSpec((1, tm, d), lambda bi, i: (bi, i, 0))
    vec = pl.BlockSpec((1, d), lambda bi, i: (0, 0))
    return pl.pallas_call(
        _rwkv_out_kernel,
        grid=(b, t // tm),
        in_specs=[row] * 5 + [
            pl.BlockSpec((1, 1, d), lambda bi, i: (bi, 0, 0)), vec, vec,
            pl.BlockSpec((d, d), lambda bi, i: (0, 0)),
            pl.BlockSpec((V7X_MXU_DIM, V7X_MXU_DIM), lambda bi, i: (0, 0)),
        ],
        out_specs=row,
        out_shape=jax.ShapeDtypeStruct((b, t, d), F32),
        compiler_params=_cparams("parallel", "parallel"),
        name="rwkv_out",
    )(x, yf, yr, bonus, gg, gate, lnx_g.reshape(1, d), lnx_b.reshape(1, d), wo, ones_bd)


def _final_norm_kernel(x_ref, g_ref, o_ref):
    x = x_ref[0]
    ms = jnp.mean(x * x, axis=-1, keepdims=True)
    o_ref[0] = x * lax.rsqrt(ms + NORM_EPS) * g_ref[...]


def _final_norm(x, g, tm_pref=512):
    b, t, d = x.shape
    tm = _row_tile(t, tm_pref)
    return pl.pallas_call(
        _final_norm_kernel,
        grid=(b, t // tm),
        in_specs=[pl.BlockSpec((1, tm, d), lambda bi, i: (bi, i, 0)), pl.BlockSpec((1, d), lambda bi, i: (0, 0))],
        out_specs=pl.BlockSpec((1, tm, d), lambda bi, i: (bi, i, 0)),
        out_shape=jax.ShapeDtypeStruct((b, t, d), F32),
        compiler_params=_cparams("parallel", "parallel"),
        name="final_norm",
    )(x, g.reshape(1, d))


def _rope_tables(t):
    n_freq = HEAD_DIM // 4
    inv = ROPE_THETA ** (-jnp.arange(n_freq, dtype=F32) / n_freq)
    pos = jnp.arange(t, dtype=jnp.int32)
    row = (pos // GRID_W).astype(F32)[:, None] * inv
    col = (pos % GRID_W).astype(F32)[:, None] * inv
    cos = jnp.concatenate([jnp.cos(row)] * 2 + [jnp.cos(col)] * 2, axis=1)
    sin = jnp.concatenate([-jnp.sin(row), jnp.sin(row), -jnp.sin(col), jnp.sin(col)], axis=1)
    return jnp.tile(cos, (1, ATTN_Q_HEADS)), jnp.tile(sin, (1, ATTN_Q_HEADS))


def _hybrid_layer(xc, xl, mc, ml, norm_g, w_in, q_g, k_g, dw, dw_b, ln_g, ln_b, w_out, tables, ones_bd):
    d = xl.shape[-1]
    o1, o2, o3 = ATTN_WIDTH, ATTN_WIDTH + KV_WIDTH, ATTN_WIDTH + 2 * KV_WIDTH
    w_perm = jnp.concatenate([w_in[:, :o1], w_in[:, o3:], w_in[:, o1:o3]], axis=1).astype(BF16)
    w_out16 = w_out.astype(BF16)
    (cos_l, sin_l), (cos_c, sin_c) = tables
    pl_ = _norm_mod_matmul(xl, norm_g, ml[0], ml[1], w_perm)
    pc_ = _norm_mod_matmul(xc, norm_g, mc[0], mc[1], w_perm)
    ql, kl, vl = _qk_prep(pl_, cos_l, sin_l, q_g, k_g, ones_bd)
    qc, kc, vc = _qk_prep(pc_, cos_c, sin_c, q_g, k_g, ones_bd)
    k_all = jnp.concatenate([kc, kl], axis=1)
    v_all = jnp.concatenate([vc, vl], axis=1)
    attn_l = _attention(ql, k_all, v_all)
    attn_c = _attention(qc, kc, vc)
    conv_l = _conformer_conv(pl_, dw, dw_b, ln_g, ln_b)
    conv_c = _conformer_conv(pc_, dw, dw_b, ln_g, ln_b)
    xl = _hybrid_out(xl, attn_l, conv_l, ml[2], w_out16)
    xc = _hybrid_out(xc, attn_c, conv_c, mc[2], w_out16)
    return xc, xl


def _rwkv_layer(xc, xl, mc, ml, norm_g, mu, wr, wk, wv, wo, w0, w1, w2, a0, a1, a2, g1, g2,
                k_k, k_a, u, lnx_g, lnx_b, ones_bd, ctx_out):
    b, _, d = xl.shape
    lora_w = w1.shape[-1]
    zeros_w = jnp.zeros((lora_w, d), F32)
    p = {
        "mu": jnp.zeros((V7X_SUBLANES, d), F32).at[:6].set(mu),
        "wr": wr.astype(BF16), "wk": wk.astype(BF16), "wv": wv.astype(BF16),
        "g1": g1.astype(BF16), "g2": g2.astype(BF16),
        "w1": jnp.concatenate([w1[0], w1[1]], axis=1).astype(BF16),
        "w2f": jnp.concatenate([w2[0], zeros_w], axis=0).astype(BF16),
        "w2r": jnp.concatenate([zeros_w, w2[1]], axis=0).astype(BF16),
        "a1": jnp.concatenate([a1[0], a1[1]], axis=1).astype(BF16),
        "a2f": jnp.concatenate([a2[0], jnp.zeros_like(a2[1])], axis=0).astype(BF16),
        "a2r": jnp.concatenate([jnp.zeros_like(a2[0]), a2[1]], axis=0).astype(BF16),
        "w0": jnp.zeros((V7X_SUBLANES, d), F32).at[:2].set(w0),
        "a0": jnp.zeros((V7X_SUBLANES, d), F32).at[:2].set(a0),
        "kk": k_k.reshape(1, d), "ka": k_a.reshape(1, d),
        "u": jnp.zeros((V7X_SUBLANES, d), F32).at[:2].set(u.reshape(2, d)),
    }
    fc = _rwkv_features(xc, norm_g, mc[0], mc[1], p, ones_bd)
    fl = _rwkv_features(xl, norm_g, ml[0], ml[1], p, ones_bd)
    ng = d // V7X_MXU_DIM
    s_zero = jnp.zeros((b, ng, V7X_MXU_DIM, V7X_MXU_DIM), F32)
    ys = {}
    for dd, rev in enumerate((False, True)):
        state = s_zero
        for name, f in (("c", fc), ("l", fl)):
            r_, v_, kkn, _, _ = f[:5]
            lw, kd, bv = f[5 + 3 * dd:8 + 3 * dd]
            y, state = _wkv_scan(r_, lw, kd, v_, kkn, bv, state, rev)
            ys[(name, dd)] = y
    wo16 = wo.astype(BF16)
    xl = _rwkv_out(xl, ys[("l", 0)], ys[("l", 1)], fl[4], fl[3], ml[2], lnx_g, lnx_b, wo16, ones_bd)
    if ctx_out:
        xc = _rwkv_out(xc, ys[("c", 0)], ys[("c", 1)], fc[4], fc[3], mc[2], lnx_g, lnx_b, wo16, ones_bd)
    return xc, xl


def kernel(x, c, ctx, c_ctx, mod_w, mod_b, norm_mix, norm_ffn, ffn_w_in, ffn_dw, ffn_dw_b, ffn_w_out, hyb_w_in, hyb_q_norm, hyb_k_norm, hyb_dw, hyb_dw_b, hyb_ln_g, hyb_ln_b, hyb_w_out, rwkv_mu, rwkv_wr, rwkv_wk, rwkv_wv, rwkv_wo, rwkv_w0, rwkv_w1, rwkv_w2, rwkv_a0, rwkv_a1, rwkv_a2, rwkv_g1, rwkv_g2, rwkv_kk, rwkv_ka, rwkv_u, rwkv_lnx_g, rwkv_lnx_b, final_norm):
    b, t, d = x.shape
    n_ctx = ctx.shape[1]
    depth = mod_w.shape[0]
    assert d % V7X_MXU_DIM == 0 and t % SCAN_CHUNK == 0 and n_ctx % SCAN_CHUNK == 0
    assert t % GRID_W == 0 and t % HALO == 0 and n_ctx % HALO == 0

    m_rows = -(-(b + 1) // V7X_SUBLANES) * V7X_SUBLANES
    cvec = jnp.zeros((m_rows, d), F32).at[:b].set(c).at[b].set(c_ctx)
    mod = _modulation(cvec, mod_w, mod_b)
    mod = mod.reshape(depth, m_rows, 6, d)

    hid = jnp.arange(V7X_MXU_DIM) // HEAD_DIM
    ones_bd = (hid[:, None] == hid[None, :]).astype(BF16)
    cos_l, sin_l = _rope_tables(t)
    tables = ((cos_l, sin_l), (jnp.ones((n_ctx, ATTN_WIDTH), F32), jnp.zeros((n_ctx, ATTN_WIDTH), F32)))

    xl, xc = x, ctx
    for i in range(depth):
        last = i == depth - 1
        j = i // 2
        ml = [mod[i, :b, n][:, None, :] for n in range(6)]
        mc = [jnp.broadcast_to(mod[i, b, n][None, None, :], (b, 1, d)) for n in range(6)]
        if i % 2 == 0:
            xc, xl = _hybrid_layer(xc, xl, mc, ml, norm_mix[i], hyb_w_in[j], hyb_q_norm[j], hyb_k_norm[j],
                                   hyb_dw[j], hyb_dw_b[j], hyb_ln_g[j], hyb_ln_b[j], hyb_w_out[j],
                                   tables, ones_bd)
        else:
            xc, xl = _rwkv_layer(xc, xl, mc, ml, norm_mix[i], rwkv_mu[j], rwkv_wr[j], rwkv_wk[j], rwkv_wv[j],
                                 rwkv_wo[j], rwkv_w0[j], rwkv_w1[j], rwkv_w2[j], rwkv_a0[j], rwkv_a1[j],
                                 rwkv_a2[j], rwkv_g1[j], rwkv_g2[j], rwkv_kk[j], rwkv_ka[j], rwkv_u[j],
                                 rwkv_lnx_g[j], rwkv_lnx_b[j], ones_bd, not last)
        w_in16 = ffn_w_in[i].astype(BF16)
        w_out16 = ffn_w_out[i].astype(BF16)
        xl = _conv_ffn(xl, norm_ffn[i], ml[3], ml[4], ml[5], w_in16, ffn_dw[i], ffn_dw_b[i], w_out16)
        if not last:
            xc = _conv_ffn(xc, norm_ffn[i], mc[3], mc[4], mc[5], w_in16, ffn_dw[i], ffn_dw_b[i], w_out16)
    return _final_norm(xl, final_norm)
```

```python
import functools

import jax
import jax.numpy as jnp
from jax import lax
from jax.experimental import pallas as pl
from jax.experimental.pallas import tpu as pltpu

F32 = jnp.float32
BF16 = jnp.bfloat16
HIGHEST = lax.Precision.HIGHEST

HEAD_DIM = 64
ATTN_Q_HEADS = 8
ATTN_KV_HEADS = 2
ATTN_GROUP = ATTN_Q_HEADS // ATTN_KV_HEADS
ATTN_WIDTH = ATTN_Q_HEADS * HEAD_DIM
KV_WIDTH = ATTN_KV_HEADS * HEAD_DIM
CONV_KERNEL = 31
GRID_W = 64
ROPE_THETA = 10000.0
NORM_EPS = 1e-6
LN_EPS = 1e-5
LNX_EPS = 64e-5

V7X_LANES = 128
V7X_SUBLANES = 8
V7X_MXU_DIM = 256
V7X_VMEM_BYTES = 64 * 1024 * 1024
VMEM_LIMIT_BYTES = 58 * 1024 * 1024

HALO = 16
SCAN_CHUNK = 64
SCAN_GROUP = V7X_MXU_DIM // HEAD_DIM
SCAN_PASSES = {"cs": 2, "sc": (1, 1), "inv": (2, 2), "sprod": (1, 1), "nv": (1, 1), "u": (1, 1),
               "abu": (1, 1), "upd": (1, 1)}


def _cparams(*sem):
    return pltpu.CompilerParams(dimension_semantics=sem, vmem_limit_bytes=VMEM_LIMIT_BYTES)


def _row_tile(t, pref):
    return pref if t % pref == 0 else t


def _dot(a, b, precision=None):
    return jnp.dot(a, b, preferred_element_type=F32, precision=precision)


def _dot_nt(a, b, precision=None):
    return lax.dot_general(a, b, (((1,), (1,)), ((), ())), preferred_element_type=F32, precision=precision)


def _dot_tn(a, b, precision=None):
    return lax.dot_general(a, b, (((0,), (0,)), ((), ())), preferred_element_type=F32, precision=precision)


def _silu(x):
    return x * jax.nn.sigmoid(x)


def _norm_mod(x, g, shift, scale):
    ms = jnp.mean(x * x, axis=-1, keepdims=True)
    return (x * lax.rsqrt(ms + NORM_EPS) * g) * (1.0 + scale) + shift


def _head_sum(x, ones_bd):
    w = x.shape[-1]
    hi = x.astype(BF16)
    lo = (x - hi.astype(F32)).astype(BF16)
    outs = []
    for s in range(0, w, V7X_MXU_DIM):
        e = min(s + V7X_MXU_DIM, w)
        g = ones_bd[: e - s, : e - s]
        outs.append(_dot(hi[:, s:e], g) + _dot(lo[:, s:e], g))
    return outs[0] if len(outs) == 1 else jnp.concatenate(outs, axis=-1)


def _mod_kernel(c_ref, w_ref, b_ref, o_ref):
    s = _silu(c_ref[...])
    o_ref[0] = _dot(s, w_ref[0], HIGHEST) + b_ref[0]


def _modulation(cvec, mod_w, mod_b):
    depth, d, n = mod_w.shape
    m = cvec.shape[0]
    tn = 512
    return pl.pallas_call(
        _mod_kernel,
        grid=(depth, n // tn),
        in_specs=[
            pl.BlockSpec((m, d), lambda l, j: (0, 0)),
            pl.BlockSpec((1, d, tn), lambda l, j: (l, 0, j)),
            pl.BlockSpec((1, 1, tn), lambda l, j: (l, 0, j)),
        ],
        out_specs=pl.BlockSpec((1, m, tn), lambda l, j: (l, 0, j)),
        out_shape=jax.ShapeDtypeStruct((depth, m, n), F32),
        compiler_params=_cparams("parallel", "parallel"),
        name="modulation",
    )(cvec, mod_w, mod_b.reshape(depth, 1, n))


def _nmm_kernel(x_ref, g_ref, sh_ref, sc_ref, w_ref, o_ref, h_ref):
    @pl.when(pl.program_id(2) == 0)
    def _():
        h_ref[...] = _norm_mod(x_ref[0], g_ref[...], sh_ref[0], sc_ref[0]).astype(BF16)

    o_ref[0] = _dot(h_ref[...], w_ref[...]).astype(o_ref.dtype)


def _norm_mod_matmul(x, g, shift, scale, w, tm_pref=512, tn=256, out_dtype=F32):
    b, t, d = x.shape
    n = w.shape[1]
    tm = _row_tile(t, tm_pref)
    return pl.pallas_call(
        _nmm_kernel,
        grid=(b, t // tm, n // tn),
        in_specs=[
            pl.BlockSpec((1, tm, d), lambda bi, i, j: (bi, i, 0)),
            pl.BlockSpec((1, d), lambda bi, i, j: (0, 0)),
            pl.BlockSpec((1, 1, d), lambda bi, i, j: (bi, 0, 0)),
            pl.BlockSpec((1, 1, d), lambda bi, i, j: (bi, 0, 0)),
            pl.BlockSpec((d, tn), lambda bi, i, j: (0, j)),
        ],
        out_specs=pl.BlockSpec((1, tm, tn), lambda bi, i, j: (bi, i, j)),
        out_shape=jax.ShapeDtypeStruct((b, t, n), out_dtype),
        scratch_shapes=[pltpu.VMEM((tm, d), BF16)],
        compiler_params=_cparams("parallel", "parallel", "arbitrary"),
        name="norm_mod_matmul",
    )(x, g.reshape(1, d), shift, scale, w)


def _ffn_kernel(x_ref, xp_ref, xn_ref, g_ref, sh_ref, sc_ref, gt_ref, wg_ref, wv_ref,
                dw_ref, db_ref, wo_ref, o_ref, h_ref, gate_ref, acc_ref, *, tm):
    i = pl.program_id(1)
    j = pl.program_id(2)
    nt = pl.num_programs(1)
    nf = pl.num_programs(2)

    @pl.when(j == 0)
    def _():
        g, sh, sc = g_ref[...], sh_ref[0], sc_ref[0]
        h_ref[0:HALO] = _norm_mod(xp_ref[0], g, sh, sc).astype(BF16)
        h_ref[HALO:HALO + tm] = _norm_mod(x_ref[0], g, sh, sc).astype(BF16)
        h_ref[HALO + tm:2 * HALO + tm] = _norm_mod(xn_ref[0], g, sh, sc).astype(BF16)
        acc_ref[...] = jnp.zeros_like(acc_ref)

    gate_ref[...] = _dot(h_ref[...], wg_ref[...])
    val = _dot(h_ref[HALO:HALO + tm], wv_ref[...])
    rows = lax.broadcasted_iota(jnp.int32, (tm, 1), 0)
    g_prev = jnp.where((rows == 0) & (i == 0), 0.0, gate_ref[pl.ds(HALO - 1, tm), :])
    g_next = jnp.where((rows == tm - 1) & (i == nt - 1), 0.0, gate_ref[pl.ds(HALO + 1, tm), :])
    conv = (g_prev * dw_ref[0:1, :] + gate_ref[pl.ds(HALO, tm), :] * dw_ref[1:2, :]
            + g_next * dw_ref[2:3, :] + db_ref[...])
    act = (_silu(conv) * val).astype(BF16)
    acc_ref[...] += _dot(act, wo_ref[...])

    @pl.when(j == nf - 1)
    def _():
        o_ref[0] = x_ref[0] + gt_ref[0] * acc_ref[...]


def _conv_ffn(x, g, shift, scale, gate, w_in, dw, dw_b, w_out, tm_pref=512):
    b, t, d = x.shape
    f = w_out.shape[0]
    nf = 2
    fc = f // nf
    tm = _row_tile(t, tm_pref)
    hb = tm // HALO
    nhb = t // HALO
    dwp = jnp.zeros((V7X_SUBLANES, f), F32).at[:dw.shape[0]].set(dw)
    return pl.pallas_call(
        functools.partial(_ffn_kernel, tm=tm),
        grid=(b, t // tm, nf),
        in_specs=[
            pl.BlockSpec((1, tm, d), lambda bi, i, j: (bi, i, 0)),
            pl.BlockSpec((1, HALO, d), lambda bi, i, j: (bi, jnp.maximum(i * hb - 1, 0), 0)),
            pl.BlockSpec((1, HALO, d), lambda bi, i, j: (bi, jnp.minimum((i + 1) * hb, nhb - 1), 0)),
            pl.BlockSpec((1, d), lambda bi, i, j: (0, 0)),
            pl.BlockSpec((1, 1, d), lambda bi, i, j: (bi, 0, 0)),
            pl.BlockSpec((1, 1, d), lambda bi, i, j: (bi, 0, 0)),
            pl.BlockSpec((1, 1, d), lambda bi, i, j: (bi, 0, 0)),
            pl.BlockSpec((d, fc), lambda bi, i, j: (0, j)),
            pl.BlockSpec((d, fc), lambda bi, i, j: (0, nf + j)),
            pl.BlockSpec((V7X_SUBLANES, fc), lambda bi, i, j: (0, j)),
            pl.BlockSpec((1, fc), lambda bi, i, j: (0, j)),
            pl.BlockSpec((fc, d), lambda bi, i, j: (j, 0)),
        ],
        out_specs=pl.BlockSpec((1, tm, d), lambda bi, i, j: (bi, i, 0)),
        out_shape=jax.ShapeDtypeStruct((b, t, d), F32),
        scratch_shapes=[
            pltpu.VMEM((tm + 2 * HALO, d), BF16),
            pltpu.VMEM((tm + 2 * HALO, fc), F32),
            pltpu.VMEM((tm, d), F32),
        ],
        compiler_params=_cparams("parallel", "parallel", "arbitrary"),
        name="conv_ffn",
    )(x, x, x, g.reshape(1, d), shift, scale, gate, w_in, w_in, dwp, dw_b.reshape(1, f), w_out)


def _rope(x, cos, sin):
    w = x.shape[-1]
    lane = lax.broadcasted_iota(jnp.int32, x.shape, x.ndim - 1)
    partner = jnp.where(lane % 32 < 16, pltpu.roll(x, w - 16, x.ndim - 1), pltpu.roll(x, 16, x.ndim - 1))
    return x * cos + partner * sin


def _qk_prep_kernel(q_ref, k_ref, v_ref, cos_ref, sin_ref, qg_ref, kg_ref, ones_ref,
                    qo_ref, ko_ref, vo_ref):
    ones_bd = ones_ref[...]
    inv = 1.0 / HEAD_DIM
    q = q_ref[0]
    q = q * lax.rsqrt(_head_sum(q * q, ones_bd) * inv + NORM_EPS) * qg_ref[...]
    q = _rope(q, cos_ref[...], sin_ref[...])
    qo_ref[0] = (q * (HEAD_DIM ** -0.5)).astype(BF16)
    k = k_ref[0]
    k = k * lax.rsqrt(_head_sum(k * k, ones_bd) * inv + NORM_EPS) * kg_ref[...]
    k = _rope(k, cos_ref[:, :KV_WIDTH], sin_ref[:, :KV_WIDTH])
    ko_ref[0] = k.astype(BF16)
    vo_ref[0] = v_ref[0].astype(BF16)


def _qk_prep(proj, cos, sin, q_g, k_g, ones_bd, tm_pref=512):
    b, t, _ = proj.shape
    tm = _row_tile(t, tm_pref)
    kcol = 3 * ATTN_WIDTH // KV_WIDTH
    return pl.pallas_call(
        _qk_prep_kernel,
        grid=(b, t // tm),
        in_specs=[
            pl.BlockSpec((1, tm, ATTN_WIDTH), lambda bi, i: (bi, i, 0)),
            pl.BlockSpec((1, tm, KV_WIDTH), lambda bi, i: (bi, i, kcol)),
            pl.BlockSpec((1, tm, KV_WIDTH), lambda bi, i: (bi, i, kcol + 1)),
            pl.BlockSpec((tm, ATTN_WIDTH), lambda bi, i: (i, 0)),
            pl.BlockSpec((tm, ATTN_WIDTH), lambda bi, i: (i, 0)),
            pl.BlockSpec((1, ATTN_WIDTH), lambda bi, i: (0, 0)),
            pl.BlockSpec((1, KV_WIDTH), lambda bi, i: (0, 0)),
            pl.BlockSpec((V7X_MXU_DIM, V7X_MXU_DIM), lambda bi, i: (0, 0)),
        ],
        out_specs=[
            pl.BlockSpec((1, tm, ATTN_WIDTH), lambda bi, i: (bi, i, 0)),
            pl.BlockSpec((1, tm, KV_WIDTH), lambda bi, i: (bi, i, 0)),
            pl.BlockSpec((1, tm, KV_WIDTH), lambda bi, i: (bi, i, 0)),
        ],
        out_shape=[
            jax.ShapeDtypeStruct((b, t, ATTN_WIDTH), BF16),
            jax.ShapeDtypeStruct((b, t, KV_WIDTH), BF16),
            jax.ShapeDtypeStruct((b, t, KV_WIDTH), BF16),
        ],
        compiler_params=_cparams("parallel", "parallel"),
        name="qk_prep",
    )(proj, proj, proj, cos, sin, jnp.tile(q_g, ATTN_Q_HEADS).reshape(1, ATTN_WIDTH),
      jnp.tile(k_g, ATTN_KV_HEADS).reshape(1, KV_WIDTH), ones_bd)


def _attn_kernel(q_ref, k_ref, v_ref, o_ref):
    for g in range(ATTN_KV_HEADS):
        kg = k_ref[0, :, g * HEAD_DIM:(g + 1) * HEAD_DIM]
        vg = v_ref[0, :, g * HEAD_DIM:(g + 1) * HEAD_DIM]
        for hh in range(ATTN_GROUP):
            lo = (g * ATTN_GROUP + hh) * HEAD_DIM
            s = _dot_nt(q_ref[0, :, lo:lo + HEAD_DIM], kg)
            m = jnp.max(s, axis=-1, keepdims=True)
            p = jnp.exp(s - m)
            l = jnp.sum(p, axis=-1, keepdims=True)
            o = _dot(p.astype(BF16), vg) / l
            o_ref[0, :, lo:lo + HEAD_DIM] = o.astype(o_ref.dtype)


def _attention(q, k, v, tq_pref=256):
    b, t, _ = q.shape
    tk = k.shape[1]
    tq = _row_tile(t, tq_pref)
    return pl.pallas_call(
        _attn_kernel,
        grid=(b, t // tq),
        in_specs=[
            pl.BlockSpec((1, tq, ATTN_WIDTH), lambda bi, i: (bi, i, 0)),
            pl.BlockSpec((1, tk, KV_WIDTH), lambda bi, i: (bi, 0, 0)),
            pl.BlockSpec((1, tk, KV_WIDTH), lambda bi, i: (bi, 0, 0)),
        ],
        out_specs=pl.BlockSpec((1, tq, ATTN_WIDTH), lambda bi, i: (bi, i, 0)),
        out_shape=jax.ShapeDtypeStruct((b, t, ATTN_WIDTH), BF16),
        compiler_params=_cparams("parallel", "parallel"),
        name="attention",
    )(q, k, v)


def _conformer_kernel(a_ref, g_ref, ap_ref, gp_ref, an_ref, gn_ref, dw_ref, db_ref, lg_ref, lb_ref,
                      o_ref, u_ref, *, tm):
    i = pl.program_id(1)
    nt = pl.num_programs(1)
    u_ref[0:HALO] = jnp.where(i == 0, 0.0, ap_ref[0] * jax.nn.sigmoid(gp_ref[0]))
    u_ref[HALO:HALO + tm] = a_ref[0] * jax.nn.sigmoid(g_ref[0])
    u_ref[HALO + tm:2 * HALO + tm] = jnp.where(i == nt - 1, 0.0, an_ref[0] * jax.nn.sigmoid(gn_ref[0]))
    half = CONV_KERNEL // 2
    acc = jnp.zeros((tm, u_ref.shape[1]), F32) + db_ref[...]
    for j in range(CONV_KERNEL):
        acc = acc + u_ref[pl.ds(HALO - half + j, tm), :] * dw_ref[j:j + 1, :]
    mean = jnp.mean(acc, axis=-1, keepdims=True)
    cen = acc - mean
    var = jnp.mean(cen * cen, axis=-1, keepdims=True)
    y = cen * lax.rsqrt(var + LN_EPS) * lg_ref[...] + lb_ref[...]
    o_ref[0] = _silu(y).astype(o_ref.dtype)


def _conformer_conv(proj, dw, dw_b, ln_g, ln_b, tm_pref=256):
    b, t, _ = proj.shape
    cw = dw.shape[1]
    tm = _row_tile(t, tm_pref)
    hb = tm // HALO
    nhb = t // HALO
    dwp = jnp.zeros((32, cw), F32).at[:CONV_KERNEL].set(dw)
    prev = lambda c: (lambda bi, i: (bi, jnp.maximum(i * hb - 1, 0), c))
    nxt = lambda c: (lambda bi, i: (bi, jnp.minimum((i + 1) * hb, nhb - 1), c))
    vec = pl.BlockSpec((1, cw), lambda bi, i: (0, 0))
    return pl.pallas_call(
        functools.partial(_conformer_kernel, tm=tm),
        grid=(b, t // tm),
        in_specs=[
            pl.BlockSpec((1, tm, cw), lambda bi, i: (bi, i, 1)),
            pl.BlockSpec((1, tm, cw), lambda bi, i: (bi, i, 2)),
            pl.BlockSpec((1, HALO, cw), prev(1)),
            pl.BlockSpec((1, HALO, cw), prev(2)),
            pl.BlockSpec((1, HALO, cw), nxt(1)),
            pl.BlockSpec((1, HALO, cw), nxt(2)),
            pl.BlockSpec((32, cw), lambda bi, i: (0, 0)),
            vec, vec, vec,
        ],
        out_specs=pl.BlockSpec((1, tm, cw), lambda bi, i: (bi, i, 0)),
        out_shape=jax.ShapeDtypeStruct((b, t, cw), BF16),
        scratch_shapes=[pltpu.VMEM((tm + 2 * HALO, cw), F32)],
        compiler_params=_cparams("parallel", "parallel"),
        name="conformer_conv",
    )(proj, proj, proj, proj, proj, proj, dwp, dw_b.reshape(1, cw), ln_g.reshape(1, cw), ln_b.reshape(1, cw))


def _hyb_out_kernel(x_ref, a_ref, c_ref, gt_ref, wa_ref, wc_ref, o_ref):
    y = _dot(a_ref[0], wa_ref[...]) + _dot(c_ref[0], wc_ref[...])
    o_ref[0] = x_ref[0] + gt_ref[0] * y


def _hybrid_out(x, attn, conv, gate, w_out, tm_pref=512):
    b, t, d = x.shape
    tm = _row_tile(t, tm_pref)
    aw = attn.shape[-1]
    cw = conv.shape[-1]
    return pl.pallas_call(
        _hyb_out_kernel,
        grid=(b, t // tm),
        in_specs=[
            pl.BlockSpec((1, tm, d), lambda bi, i: (bi, i, 0)),
            pl.BlockSpec((1, tm, aw), lambda bi, i: (bi, i, 0)),
            pl.BlockSpec((1, tm, cw), lambda bi, i: (bi, i, 0)),
            pl.BlockSpec((1, 1, d), lambda bi, i: (bi, 0, 0)),
            pl.BlockSpec((aw, d), lambda bi, i: (0, 0)),
            pl.BlockSpec((cw, d), lambda bi, i: (0, 0)),
        ],
        out_specs=pl.BlockSpec((1, tm, d), lambda bi, i: (bi, i, 0)),
        out_shape=jax.ShapeDtypeStruct((b, t, d), F32),
        compiler_params=_cparams("parallel", "parallel"),
        name="hybrid_out",
    )(x, attn, conv, gate, w_out[:aw], w_out[aw:])


def _rwkv_feat_kernel(x_ref, xp_ref, xn_ref, g_ref, sh_ref, sc_ref, mu_ref, wr_ref, wk_ref, wv_ref,
                      g1_ref, g2_ref, w1_ref, w2f_ref, w2r_ref, a1_ref, a2f_ref, a2r_ref,
                      w0_ref, a0_ref, kk_ref, ka_ref, u_ref, ones_ref,
                      r_o, v_o, kkn_o, gg_o, bonus_o, lwf_o, kdf_o, bf_o, lwr_o, kdr_o, br_o, *, tm):
    i = pl.program_id(1)
    nt = pl.num_programs(1)
    g, sh, sc = g_ref[...], sh_ref[0], sc_ref[0]
    h = _norm_mod(x_ref[0], g, sh, sc)
    hp = jnp.where(i == 0, 0.0, _norm_mod(xp_ref[0, HALO - 1:HALO, :], g, sh, sc))
    hn = jnp.where(i == nt - 1, 0.0, _norm_mod(xn_ref[0, 0:1, :], g, sh, sc))
    rows = lax.broadcasted_iota(jnp.int32, (tm, 1), 0)
    up = jnp.where(rows == 0, hp, pltpu.roll(h, 1, 0))
    dn = jnp.where(rows == tm - 1, hn, pltpu.roll(h, tm - 1, 0))
    xx = 0.5 * (up + dn) - h

    def mix(n):
        return (h + xx * mu_ref[n:n + 1, :]).astype(BF16)

    r = _dot(mix(0), wr_ref[...])
    k = _dot(mix(2), wk_ref[...])
    v = _dot(mix(3), wv_ref[...])
    gg = _dot(jax.nn.sigmoid(_dot(mix(5), g1_ref[...])).astype(BF16), g2_ref[...])
    tl = jnp.tanh(_dot(mix(1), w1_ref[...])).astype(BF16)
    al = _dot(mix(4), a1_ref[...]).astype(BF16)

    ones_bd = ones_ref[...]
    kkf = k * kk_ref[...]
    kkn = kkf / jnp.maximum(jnp.sqrt(_head_sum(kkf * kkf, ones_bd)), 1e-12)
    r_o[0] = r
    v_o[0] = v
    kkn_o[0] = kkn
    gg_o[0] = gg

    bonus = jnp.zeros_like(r)
    outs = ((w2f_ref, a2f_ref, lwf_o, kdf_o, bf_o), (w2r_ref, a2r_ref, lwr_o, kdr_o, br_o))
    for dd, (w2_ref, a2_ref, lw_o, kd_o, b_o) in enumerate(outs):
        z = -(w0_ref[dd:dd + 1, :] + _dot(tl, w2_ref[...]))
        softplus = jnp.maximum(z, 0.0) + jnp.log1p(jnp.exp(-jnp.abs(z)))
        lw_o[0] = -jnp.exp(-softplus - 0.5)
        a = jax.nn.sigmoid(a0_ref[dd:dd + 1, :] + _dot(al, a2_ref[...]))
        kd = k * (1.0 + (a - 1.0) * ka_ref[...])
        kd_o[0] = kd
        b_o[0] = kkn * a
        bonus = bonus + _head_sum(r * kd * u_ref[dd:dd + 1, :], ones_bd) * v
    bonus_o[0] = bonus


def _rwkv_features(x, g, shift, scale, p, ones_bd, tm_pref=256):
    b, t, d = x.shape
    tm = _row_tile(t, tm_pref)
    hb = tm // HALO
    nhb = t // HALO
    row = pl.BlockSpec((1, tm, d), lambda bi, i: (bi, i, 0))
    vec3 = pl.BlockSpec((1, 1, d), lambda bi, i: (bi, 0, 0))

    def full(a):
        return pl.BlockSpec(a.shape, lambda bi, i: (0,) * a.ndim)

    consts = [p["mu"], p["wr"], p["wk"], p["wv"], p["g1"], p["g2"], p["w1"], p["w2f"], p["w2r"],
              p["a1"], p["a2f"], p["a2r"], p["w0"], p["a0"], p["kk"], p["ka"], p["u"], ones_bd]
    n_out = 11
    return pl.pallas_call(
        functools.partial(_rwkv_feat_kernel, tm=tm),
        grid=(b, t // tm),
        in_specs=[
            row,
            pl.BlockSpec((1, HALO, d), lambda bi, i: (bi, jnp.maximum(i * hb - 1, 0), 0)),
            pl.BlockSpec((1, HALO, d), lambda bi, i: (bi, jnp.minimum((i + 1) * hb, nhb - 1), 0)),
            pl.BlockSpec((1, d), lambda bi, i: (0, 0)),
            vec3, vec3,
        ] + [full(a) for a in consts],
        out_specs=[row] * n_out,
        out_shape=[jax.ShapeDtypeStruct((b, t, d), F32)] * n_out,
        compiler_params=_cparams("parallel", "parallel"),
        name="rwkv_features",
    )(x, x, x, g.reshape(1, d), shift, scale, *consts)


def _split_bf16(x, n):
    parts, rem = [], x
    for _ in range(n):
        p = rem.astype(BF16)
        parts.append(p)
        rem = rem - p.astype(F32)
    return parts


def _mm_parts(a_parts, b_parts, fn):
    n = max(len(a_parts), len(b_parts))
    acc = None
    for i in reversed(range(len(a_parts))):
        for j in reversed(range(len(b_parts))):
            if i + j < n:
                term = fn(a_parts[i], b_parts[j])
                acc = term if acc is None else acc + term
    return acc


def _bdot(a, b):
    return lax.dot_general(a, b, (((2,), (1,)), ((0,), (0,))), preferred_element_type=F32)


def _bdot_nt(a, b):
    return lax.dot_general(a, b, (((2,), (2,)), ((0,), (0,))), preferred_element_type=F32)


def _bdot_tn(a, b):
    return lax.dot_general(a, b, (((1,), (1,)), ((0,), (0,))), preferred_element_type=F32)


def _scan_kernel(r_ref, lw_ref, kd_ref, v_ref, kk_ref, b_ref, s0_ref, ms_ref, mi_ref, bd_ref, eye_ref,
                 y_ref, sf_ref, st_ref, *, reverse, passes):
    c = pl.program_id(1)
    nc = pl.num_programs(1)
    L = r_ref.shape[1]
    W = V7X_MXU_DIM
    G = r_ref.shape[2] // W

    @pl.when(c == 0)
    def _():
        st_ref[...] = s0_ref[0]

    mask_strict = ms_ref[...][None]
    mask_incl = mi_ref[...][None]
    bd = bd_ref[...][None]
    bd16 = bd.astype(BF16)
    eye = eye_ref[...]
    tri16 = mi_ref[:, :L].astype(BF16)
    eye_row = jnp.concatenate([eye[:L, :L]] * SCAN_GROUP, axis=1)[None]
    last = 0 if reverse else L - 1

    def groups(a):
        return jnp.stack([a[:, g * W:(g + 1) * W] for g in range(G)], axis=0)

    def expand(parts):
        return [jnp.concatenate([p] * SCAN_GROUP, axis=1) * bd16 for p in parts]

    def cat(parts_a, parts_b):
        return [jnp.concatenate([a, b], axis=1) for a, b in zip(parts_a, parts_b)]

    lw = lw_ref[0]
    cs = _mm_parts([tri16], _split_bf16(lw, passes["cs"]), _dot)
    gam = jnp.exp(cs)
    gam_inv = jnp.exp(-cs)
    gam_end = gam[last:last + 1, :]
    kkg = groups(kk_ref[0] * jnp.exp(cs - lw))
    rg = groups(r_ref[0] * gam)
    kt = groups(kd_ref[0] * gam_inv)
    bt = groups(b_ref[0] * gam_inv)
    kb = groups(jnp.concatenate([kd_ref[0] * gam_inv * gam_end, -(b_ref[0] * gam_inv * gam_end)], axis=0))
    v = groups(v_ref[0])
    st = st_ref[...]

    pa, pb = passes["sc"]
    lhs = jnp.concatenate([kkg, rg], axis=1)
    lhs_p = _split_bf16(lhs, max(pa, passes["sprod"][0]))
    rhs_p = cat(expand(_split_bf16(bt, pb)), expand(_split_bf16(kt, pb)))
    sc = _mm_parts(lhs_p[:pa], rhs_p, _bdot_nt)
    m_row = sc[:, :L, :W] * mask_strict
    n_row = sc[:, :L, W:] * mask_strict
    ab_row = sc[:, L:, :W] * mask_incl
    ak_row = sc[:, L:, W:] * mask_incl

    pa, pb = passes["inv"]
    q = -m_row
    t_row = eye_row + q
    q_p = _split_bf16(q, max(pa, pb))
    q = _mm_parts(q_p[:pa], expand(q_p[:pb]), _bdot)
    span = 2
    while span < L:
        q_p = _split_bf16(q, max(pa, pb))
        if 2 * span < L:
            both = _mm_parts(cat(_split_bf16(t_row, pa), q_p[:pa]), expand(q_p[:pb]), _bdot)
            t_row = t_row + both[:, :L]
            q = both[:, L:]
        else:
            t_row = t_row + _mm_parts(_split_bf16(t_row, pa), expand(q_p[:pb]), _bdot)
        span *= 2

    pa, pb = passes["sprod"]
    s_prod = _mm_parts(lhs_p[:pa], _split_bf16(st, pb), _bdot)
    pa, pb = passes["nv"]
    v_p = _split_bf16(v, max(pb, passes["upd"][1]))
    nv = _mm_parts(_split_bf16(jnp.concatenate([n_row, ak_row], axis=1), pa), expand(v_p[:pb]), _bdot)
    pa, pb = passes["u"]
    u = _mm_parts(_split_bf16(t_row, pa), expand(_split_bf16(s_prod[:, :L] + nv[:, :L], pb)), _bdot)
    pa, pb = passes["abu"]
    u_p = _split_bf16(u, max(pb, passes["upd"][1]))
    y = s_prod[:, L:] + nv[:, L:] - _mm_parts(_split_bf16(ab_row, pa), expand(u_p[:pb]), _bdot)
    for g in range(G):
        y_ref[0, :, g * W:(g + 1) * W] = y[g]

    pa, pb = passes["upd"]
    upd = _mm_parts(_split_bf16(kb, pa), cat(v_p[:pb], u_p[:pb]), _bdot_tn) * bd
    gcol = jnp.sum(eye[None] * groups(gam_end), axis=2, keepdims=True)
    st_ref[...] = st * gcol + upd

    @pl.when(c == nc - 1)
    def _():
        sf_ref[0] = st_ref[...]


def _scan_masks(L, reverse):
    t = jnp.arange(L)[:, None]
    i = jnp.arange(L)[None, :]
    strict = (i > t) if reverse else (i < t)
    incl = (i >= t) if reverse else (i <= t)
    tile = lambda m: jnp.tile(m.astype(F32), (1, SCAN_GROUP))
    return tile(strict), tile(incl)


def _wkv_scan(r, lw, kd, v, kk, bvec, s0, reverse):
    b, t, d = r.shape
    L = SCAN_CHUNK
    nc = t // L
    W = V7X_MXU_DIM
    ng = d // W
    ms, mi = _scan_masks(L, reverse)
    hid = jnp.arange(W) // HEAD_DIM
    bd = (hid[:, None] == hid[None, :]).astype(F32)
    eye = jnp.eye(W, dtype=F32)
    cidx = (lambda bi, c: (bi, nc - 1 - c, 0)) if reverse else (lambda bi, c: (bi, c, 0))
    row = pl.BlockSpec((1, L, d), cidx)
    state = pl.BlockSpec((1, ng, W, W), lambda bi, c: (bi, 0, 0, 0))
    const = lambda a: pl.BlockSpec(a.shape, lambda bi, c: (0, 0))
    return pl.pallas_call(
        functools.partial(_scan_kernel, reverse=reverse, passes=SCAN_PASSES),
        grid=(b, nc),
        in_specs=[row] * 6 + [state, const(ms), const(mi), const(bd), const(eye)],
        out_specs=[row, state],
        out_shape=[jax.ShapeDtypeStruct((b, t, d), F32), jax.ShapeDtypeStruct((b, ng, W, W), F32)],
        scratch_shapes=[pltpu.VMEM((ng, W, W), F32)],
        compiler_params=_cparams("parallel", "arbitrary"),
        name="wkv_scan_rev" if reverse else "wkv_scan_fwd",
    )(r, lw, kd, v, kk, bvec, s0, ms, mi, bd, eye)


def _rwkv_out_kernel(x_ref, yf_ref, yr_ref, bonus_ref, gg_ref, gt_ref, lg_ref, lb_ref, wo_ref, ones_ref, o_ref):
    ones_bd = ones_ref[...]
    inv = 1.0 / HEAD_DIM
    y = yf_ref[0] + yr_ref[0]
    cen = y - _head_sum(y, ones_bd) * inv
    var = _head_sum(cen * cen, ones_bd) * inv
    yn = cen * lax.rsqrt(var + LNX_EPS) * lg_ref[...] + lb_ref[...]
    z = ((yn + bonus_ref[0]) * gg_ref[0]).astype(BF16)
    o_ref[0] = x_ref[0] + gt_ref[0] * _dot(z, wo_ref[...])


def _rwkv_out(x, yf, yr, bonus, gg, gate, lnx_g, lnx_b, wo, ones_bd, tm_pref=256):
    b, t, d = x.shape
    tm = _row_tile(t, tm_pref)
    row = pl.BlockSpec((1, tm, d), lambda bi, i: (bi, i, 0))
    vec = pl.BlockSpec((1, d), lambda bi, i: (0, 0))
    return pl.pallas_call(
        _rwkv_out_kernel,
        grid=(b, t // tm),
        in_specs=[row] * 5 + [
            pl.BlockSpec((1, 1, d), lambda bi, i: (bi, 0, 0)), vec, vec,
            pl.BlockSpec((d, d), lambda bi, i: (0, 0)),
            pl.BlockSpec((V7X_MXU_DIM, V7X_MXU_DIM), lambda bi, i: (0, 0)),
        ],
        out_specs=row,
        out_shape=jax.ShapeDtypeStruct((b, t, d), F32),
        compiler_params=_cparams("parallel", "parallel"),
        name="rwkv_out",
    )(x, yf, yr, bonus, gg, gate, lnx_g.reshape(1, d), lnx_b.reshape(1, d), wo, ones_bd)


def _final_norm_kernel(x_ref, g_ref, o_ref):
    x = x_ref[0]
    ms = jnp.mean(x * x, axis=-1, keepdims=True)
    o_ref[0] = x * lax.rsqrt(ms + NORM_EPS) * g_ref[...]


def _final_norm(x, g, tm_pref=512):
    b, t, d = x.shape
    tm = _row_tile(t, tm_pref)
    return pl.pallas_call(
        _final_norm_kernel,
        grid=(b, t // tm),
        in_specs=[pl.BlockSpec((1, tm, d), lambda bi, i: (bi, i, 0)), pl.BlockSpec((1, d), lambda bi, i: (0, 0))],
        out_specs=pl.BlockSpec((1, tm, d), lambda bi, i: (bi, i, 0)),
        out_shape=jax.ShapeDtypeStruct((b, t, d), F32),
        compiler_params=_cparams("parallel", "parallel"),
        name="final_norm",
    )(x, g.reshape(1, d))


def _rope_tables(t):
    n_freq = HEAD_DIM // 4
    inv = ROPE_THETA ** (-jnp.arange(n_freq, dtype=F32) / n_freq)
    pos = jnp.arange(t, dtype=jnp.int32)
    row = (pos // GRID_W).astype(F32)[:, None] * inv
    col = (pos % GRID_W).astype(F32)[:, None] * inv
    cos = jnp.concatenate([jnp.cos(row)] * 2 + [jnp.cos(col)] * 2, axis=1)
    sin = jnp.concatenate([-jnp.sin(row), jnp.sin(row), -jnp.sin(col), jnp.sin(col)], axis=1)
    return jnp.tile(cos, (1, ATTN_Q_HEADS)), jnp.tile(sin, (1, ATTN_Q_HEADS))


def _hybrid_layer(xc, xl, mc, ml, norm_g, w_in, q_g, k_g, dw, dw_b, ln_g, ln_b, w_out, tables, ones_bd):
    d = xl.shape[-1]
    o1, o2, o3 = ATTN_WIDTH, ATTN_WIDTH + KV_WIDTH, ATTN_WIDTH + 2 * KV_WIDTH
    w_perm = jnp.concatenate([w_in[:, :o1], w_in[:, o3:], w_in[:, o1:o3]], axis=1).astype(BF16)
    w_out16 = w_out.astype(BF16)
    (cos_l, sin_l), (cos_c, sin_c) = tables
    pl_ = _norm_mod_matmul(xl, norm_g, ml[0], ml[1], w_perm)
    pc_ = _norm_mod_matmul(xc, norm_g, mc[0], mc[1], w_perm)
    ql, kl, vl = _qk_prep(pl_, cos_l, sin_l, q_g, k_g, ones_bd)
    qc, kc, vc = _qk_prep(pc_, cos_c, sin_c, q_g, k_g, ones_bd)
    k_all = jnp.concatenate([kc, kl], axis=1)
    v_all = jnp.concatenate([vc, vl], axis=1)
    attn_l = _attention(ql, k_all, v_all)
    attn_c = _attention(qc, kc, vc)
    conv_l = _conformer_conv(pl_, dw, dw_b, ln_g, ln_b)
    conv_c = _conformer_conv(pc_, dw, dw_b, ln_g, ln_b)
    xl = _hybrid_out(xl, attn_l, conv_l, ml[2], w_out16)
    xc = _hybrid_out(xc, attn_c, conv_c, mc[2], w_out16)
    return xc, xl


def _rwkv_layer(xc, xl, mc, ml, norm_g, mu, wr, wk, wv, wo, w0, w1, w2, a0, a1, a2, g1, g2,
                k_k, k_a, u, lnx_g, lnx_b, ones_bd, ctx_out):
    b, _, d = xl.shape
    lora_w = w1.shape[-1]
    zeros_w = jnp.zeros((lora_w, d), F32)
    p = {
        "mu": jnp.zeros((V7X_SUBLANES, d), F32).at[:6].set(mu),
        "wr": wr.astype(BF16), "wk": wk.astype(BF16), "wv": wv.astype(BF16),
        "g1": g1.astype(BF16), "g2": g2.astype(BF16),
        "w1": jnp.concatenate([w1[0], w1[1]], axis=1).astype(BF16),
        "w2f": jnp.concatenate([w2[0], zeros_w], axis=0).astype(BF16),
        "w2r": jnp.concatenate([zeros_w, w2[1]], axis=0).astype(BF16),
        "a1": jnp.concatenate([a1[0], a1[1]], axis=1).astype(BF16),
        "a2f": jnp.concatenate([a2[0], jnp.zeros_like(a2[1])], axis=0).astype(BF16),
        "a2r": jnp.concatenate([jnp.zeros_like(a2[0]), a2[1]], axis=0).astype(BF16),
        "w0": jnp.zeros((V7X_SUBLANES, d), F32).at[:2].set(w0),
        "a0": jnp.zeros((V7X_SUBLANES, d), F32).at[:2].set(a0),
        "kk": k_k.reshape(1, d), "ka": k_a.reshape(1, d),
        "u": jnp.zeros((V7X_SUBLANES, d), F32).at[:2].set(u.reshape(2, d)),
    }
    fc = _rwkv_features(xc, norm_g, mc[0], mc[1], p, ones_bd)
    fl = _rwkv_features(xl, norm_g, ml[0], ml[1], p, ones_bd)
    ng = d // V7X_MXU_DIM
    s_zero = jnp.zeros((b, ng, V7X_MXU_DIM, V7X_MXU_DIM), F32)
    ys = {}
    for dd, rev in enumerate((False, True)):
        state = s_zero
        for name, f in (("c", fc), ("l", fl)):
            r_, v_, kkn, _, _ = f[:5]
            lw, kd, bv = f[5 + 3 * dd:8 + 3 * dd]
            y, state = _wkv_scan(r_, lw, kd, v_, kkn, bv, state, rev)
            ys[(name, dd)] = y
    wo16 = wo.astype(BF16)
    xl = _rwkv_out(xl, ys[("l", 0)], ys[("l", 1)], fl[4], fl[3], ml[2], lnx_g, lnx_b, wo16, ones_bd)
    if ctx_out:
        xc = _rwkv_out(xc, ys[("c", 0)], ys[("c", 1)], fc[4], fc[3], mc[2], lnx_g, lnx_b, wo16, ones_bd)
    return xc, xl


def kernel(x, c, ctx, c_ctx, mod_w, mod_b, norm_mix, norm_ffn, ffn_w_in, ffn_dw, ffn_dw_b, ffn_w_out, hyb_w_in, hyb_q_norm, hyb_k_norm, hyb_dw, hyb_dw_b, hyb_ln_g, hyb_ln_b, hyb_w_out, rwkv_mu, rwkv_wr, rwkv_wk, rwkv_wv, rwkv_wo, rwkv_w0, rwkv_w1, rwkv_w2, rwkv_a0, rwkv_a1, rwkv_a2, rwkv_g1, rwkv_g2, rwkv_kk, rwkv_ka, rwkv_u, rwkv_lnx_g, rwkv_lnx_b, final_norm):
    b, t, d = x.shape
    n_ctx = ctx.shape[1]
    depth = mod_w.shape[0]
    assert d % V7X_MXU_DIM == 0 and t % SCAN_CHUNK == 0 and n_ctx % SCAN_CHUNK == 0
    assert t % GRID_W == 0 and t % HALO == 0 and n_ctx % HALO == 0

    m_rows = -(-(b + 1) // V7X_SUBLANES) * V7X_SUBLANES
    cvec = jnp.zeros((m_rows, d), F32).at[:b].set(c).at[b].set(c_ctx)
    mod = _modulation(cvec, mod_w, mod_b)
    mod = mod.reshape(depth, m_rows, 6, d)

    hid = jnp.arange(V7X_MXU_DIM) // HEAD_DIM
    ones_bd = (hid[:, None] == hid[None, :]).astype(BF16)
    cos_l, sin_l = _rope_tables(t)
    tables = ((cos_l, sin_l), (jnp.ones((n_ctx, ATTN_WIDTH), F32), jnp.zeros((n_ctx, ATTN_WIDTH), F32)))

    xl, xc = x, ctx
    for i in range(depth):
        last = i == depth - 1
        j = i // 2
        ml = [mod[i, :b, n][:, None, :] for n in range(6)]
        mc = [jnp.broadcast_to(mod[i, b, n][None, None, :], (b, 1, d)) for n in range(6)]
        if i % 2 == 0:
            xc, xl = _hybrid_layer(xc, xl, mc, ml, norm_mix[i], hyb_w_in[j], hyb_q_norm[j], hyb_k_norm[j],
                                   hyb_dw[j], hyb_dw_b[j], hyb_ln_g[j], hyb_ln_b[j], hyb_w_out[j],
                                   tables, ones_bd)
        else:
            xc, xl = _rwkv_layer(xc, xl, mc, ml, norm_mix[i], rwkv_mu[j], rwkv_wr[j], rwkv_wk[j], rwkv_wv[j],
                                 rwkv_wo[j], rwkv_w0[j], rwkv_w1[j], rwkv_w2[j], rwkv_a0[j], rwkv_a1[j],
                                 rwkv_a2[j], rwkv_g1[j], rwkv_g2[j], rwkv_kk[j], rwkv_ka[j], rwkv_u[j],
                                 rwkv_lnx_g[j], rwkv_lnx_b[j], ones_bd, not last)
        w_in16 = ffn_w_in[i].astype(BF16)
        w_out16 = ffn_w_out[i].astype(BF16)
        xl = _conv_ffn(xl, norm_ffn[i], ml[3], ml[4], ml[5], w_in16, ffn_dw[i], ffn_dw_b[i], w_out16)
        if not last:
            xc = _conv_ffn(xc, norm_ffn[i], mc[3], mc[4], mc[5], w_in16, ffn_dw[i], ffn_dw_b[i], w_out16)
    return _final_norm(xl, final_norm)
```

```python
import functools

import jax
import jax.numpy as jnp
from jax import lax
from jax.experimental import pallas as pl
from jax.experimental.pallas import tpu as pltpu

F32 = jnp.float32
BF16 = jnp.bfloat16
HIGHEST = lax.Precision.HIGHEST

HEAD_DIM = 64
ATTN_Q_HEADS = 8
ATTN_KV_HEADS = 2
ATTN_GROUP = ATTN_Q_HEADS // ATTN_KV_HEADS
ATTN_WIDTH = ATTN_Q_HEADS * HEAD_DIM
KV_WIDTH = ATTN_KV_HEADS * HEAD_DIM
CONV_KERNEL = 31
GRID_W = 64
ROPE_THETA = 10000.0
NORM_EPS = 1e-6
LN_EPS = 1e-5
LNX_EPS = 64e-5

V7X_LANES = 128
V7X_SUBLANES = 8
V7X_MXU_DIM = 256
V7X_VMEM_BYTES = 64 * 1024 * 1024
VMEM_LIMIT_BYTES = 58 * 1024 * 1024

HALO = 16
SCAN_CHUNK = 64
SCAN_GROUP = V7X_MXU_DIM // HEAD_DIM
SCAN_PASSES = {"cs": 2, "sc": (1, 1), "inv": (1, 1), "sprod": (1, 1), "nv": (1, 1), "u": (1, 1),
               "abu": (1, 1), "upd": (1, 1)}


def _cparams(*sem):
    return pltpu.CompilerParams(dimension_semantics=sem, vmem_limit_bytes=VMEM_LIMIT_BYTES)


def _row_tile(t, pref):
    return pref if t % pref == 0 else t


def _dot(a, b, precision=None):
    return jnp.dot(a, b, preferred_element_type=F32, precision=precision)


def _dot_nt(a, b, precision=None):
    return lax.dot_general(a, b, (((1,), (1,)), ((), ())), preferred_element_type=F32, precision=precision)


def _dot_tn(a, b, precision=None):
    return lax.dot_general(a, b, (((0,), (0,)), ((), ())), preferred_element_type=F32, precision=precision)


def _silu(x):
    return x * jax.nn.sigmoid(x)


def _norm_mod(x, g, shift, scale):
    ms = jnp.mean(x * x, axis=-1, keepdims=True)
    return (x * lax.rsqrt(ms + NORM_EPS) * g) * (1.0 + scale) + shift


def _head_sum(x, ones_bd):
    w = x.shape[-1]
    hi = x.astype(BF16)
    lo = (x - hi.astype(F32)).astype(BF16)
    outs = []
    for s in range(0, w, V7X_MXU_DIM):
        e = min(s + V7X_MXU_DIM, w)
        g = ones_bd[: e - s, : e - s]
        outs.append(_dot(hi[:, s:e], g) + _dot(lo[:, s:e], g))
    return outs[0] if len(outs) == 1 else jnp.concatenate(outs, axis=-1)


def _mod_kernel(c_ref, w_ref, b_ref, o_ref):
    s = _silu(c_ref[...])
    o_ref[0] = _dot(s, w_ref[0], HIGHEST) + b_ref[0]


def _modulation(cvec, mod_w, mod_b):
    depth, d, n = mod_w.shape
    m = cvec.shape[0]
    tn = 512
    return pl.pallas_call(
        _mod_kernel,
        grid=(depth, n // tn),
        in_specs=[
            pl.BlockSpec((m, d), lambda l, j: (0, 0)),
            pl.BlockSpec((1, d, tn), lambda l, j: (l, 0, j)),
            pl.BlockSpec((1, 1, tn), lambda l, j: (l, 0, j)),
        ],
        out_specs=pl.BlockSpec((1, m, tn), lambda l, j: (l, 0, j)),
        out_shape=jax.ShapeDtypeStruct((depth, m, n), F32),
        compiler_params=_cparams("parallel", "parallel"),
        name="modulation",
    )(cvec, mod_w, mod_b.reshape(depth, 1, n))


def _nmm_kernel(x_ref, g_ref, sh_ref, sc_ref, w_ref, o_ref, h_ref):
    @pl.when(pl.program_id(2) == 0)
    def _():
        h_ref[...] = _norm_mod(x_ref[0], g_ref[...], sh_ref[0], sc_ref[0]).astype(BF16)

    o_ref[0] = _dot(h_ref[...], w_ref[...]).astype(o_ref.dtype)


def _norm_mod_matmul(x, g, shift, scale, w, tm_pref=512, tn=256, out_dtype=F32):
    b, t, d = x.shape
    n = w.shape[1]
    tm = _row_tile(t, tm_pref)
    return pl.pallas_call(
        _nmm_kernel,
        grid=(b, t // tm, n // tn),
        in_specs=[
            pl.BlockSpec((1, tm, d), lambda bi, i, j: (bi, i, 0)),
            pl.BlockSpec((1, d), lambda bi, i, j: (0, 0)),
            pl.BlockSpec((1, 1, d), lambda bi, i, j: (bi, 0, 0)),
            pl.BlockSpec((1, 1, d), lambda bi, i, j: (bi, 0, 0)),
            pl.BlockSpec((d, tn), lambda bi, i, j: (0, j)),
        ],
        out_specs=pl.BlockSpec((1, tm, tn), lambda bi, i, j: (bi, i, j)),
        out_shape=jax.ShapeDtypeStruct((b, t, n), out_dtype),
        scratch_shapes=[pltpu.VMEM((tm, d), BF16)],
        compiler_params=_cparams("parallel", "parallel", "arbitrary"),
        name="norm_mod_matmul",
    )(x, g.reshape(1, d), shift, scale, w)


def _ffn_kernel(x_ref, xp_ref, xn_ref, g_ref, sh_ref, sc_ref, gt_ref, wg_ref, wv_ref,
                dw_ref, db_ref, wo_ref, o_ref, h_ref, gate_ref, acc_ref, *, tm):
    i = pl.program_id(1)
    j = pl.program_id(2)
    nt = pl.num_programs(1)
    nf = pl.num_programs(2)

    @pl.when(j == 0)
    def _():
        g, sh, sc = g_ref[...], sh_ref[0], sc_ref[0]
        h_ref[0:HALO] = _norm_mod(xp_ref[0], g, sh, sc).astype(BF16)
        h_ref[HALO:HALO + tm] = _norm_mod(x_ref[0], g, sh, sc).astype(BF16)
        h_ref[HALO + tm:2 * HALO + tm] = _norm_mod(xn_ref[0], g, sh, sc).astype(BF16)
        acc_ref[...] = jnp.zeros_like(acc_ref)

    gate_ref[...] = _dot(h_ref[...], wg_ref[...])
    val = _dot(h_ref[HALO:HALO + tm], wv_ref[...])
    rows = lax.broadcasted_iota(jnp.int32, (tm, 1), 0)
    g_prev = jnp.where((rows == 0) & (i == 0), 0.0, gate_ref[pl.ds(HALO - 1, tm), :])
    g_next = jnp.where((rows == tm - 1) & (i == nt - 1), 0.0, gate_ref[pl.ds(HALO + 1, tm), :])
    conv = (g_prev * dw_ref[0:1, :] + gate_ref[pl.ds(HALO, tm), :] * dw_ref[1:2, :]
            + g_next * dw_ref[2:3, :] + db_ref[...])
    act = (_silu(conv) * val).astype(BF16)
    acc_ref[...] += _dot(act, wo_ref[...])

    @pl.when(j == nf - 1)
    def _():
        o_ref[0] = x_ref[0] + gt_ref[0] * acc_ref[...]


def _conv_ffn(x, g, shift, scale, gate, w_in, dw, dw_b, w_out, tm_pref=512):
    b, t, d = x.shape
    f = w_out.shape[0]
    nf = 2
    fc = f // nf
    tm = _row_tile(t, tm_pref)
    hb = tm // HALO
    nhb = t // HALO
    dwp = jnp.zeros((V7X_SUBLANES, f), F32).at[:dw.shape[0]].set(dw)
    return pl.pallas_call(
        functools.partial(_ffn_kernel, tm=tm),
        grid=(b, t // tm, nf),
        in_specs=[
            pl.BlockSpec((1, tm, d), lambda bi, i, j: (bi, i, 0)),
            pl.BlockSpec((1, HALO, d), lambda bi, i, j: (bi, jnp.maximum(i * hb - 1, 0), 0)),
            pl.BlockSpec((1, HALO, d), lambda bi, i, j: (bi, jnp.minimum((i + 1) * hb, nhb - 1), 0)),
            pl.BlockSpec((1, d), lambda bi, i, j: (0, 0)),
            pl.BlockSpec((1, 1, d), lambda bi, i, j: (bi, 0, 0)),
            pl.BlockSpec((1, 1, d), lambda bi, i, j: (bi, 0, 0)),
            pl.BlockSpec((1, 1, d), lambda bi, i, j: (bi, 0, 0)),
            pl.BlockSpec((d, fc), lambda bi, i, j: (0, j)),
            pl.BlockSpec((d, fc), lambda bi, i, j: (0, nf + j)),
            pl.BlockSpec((V7X_SUBLANES, fc), lambda bi, i, j: (0, j)),
            pl.BlockSpec((1, fc), lambda bi, i, j: (0, j)),
            pl.BlockSpec((fc, d), lambda bi, i, j: (j, 0)),
        ],
        out_specs=pl.BlockSpec((1, tm, d), lambda bi, i, j: (bi, i, 0)),
        out_shape=jax.ShapeDtypeStruct((b, t, d), F32),
        scratch_shapes=[
            pltpu.VMEM((tm + 2 * HALO, d), BF16),
            pltpu.VMEM((tm + 2 * HALO, fc), F32),
            pltpu.VMEM((tm, d), F32),
        ],
        compiler_params=_cparams("parallel", "parallel", "arbitrary"),
        name="conv_ffn",
    )(x, x, x, g.reshape(1, d), shift, scale, gate, w_in, w_in, dwp, dw_b.reshape(1, f), w_out)


def _rope(x, cos, sin):
    w = x.shape[-1]
    lane = lax.broadcasted_iota(jnp.int32, x.shape, x.ndim - 1)
    partner = jnp.where(lane % 32 < 16, pltpu.roll(x, w - 16, x.ndim - 1), pltpu.roll(x, 16, x.ndim - 1))
    return x * cos + partner * sin


def _qk_prep_kernel(q_ref, k_ref, v_ref, cos_ref, sin_ref, qg_ref, kg_ref, ones_ref,
                    qo_ref, ko_ref, vo_ref):
    ones_bd = ones_ref[...]
    inv = 1.0 / HEAD_DIM
    q = q_ref[0]
    q = q * lax.rsqrt(_head_sum(q * q, ones_bd) * inv + NORM_EPS) * qg_ref[...]
    q = _rope(q, cos_ref[...], sin_ref[...])
    qo_ref[0] = (q * (HEAD_DIM ** -0.5)).astype(BF16)
    k = k_ref[0]
    k = k * lax.rsqrt(_head_sum(k * k, ones_bd) * inv + NORM_EPS) * kg_ref[...]
    k = _rope(k, cos_ref[:, :KV_WIDTH], sin_ref[:, :KV_WIDTH])
    ko_ref[0] = k.astype(BF16)
    vo_ref[0] = v_ref[0].astype(BF16)


def _qk_prep(proj, cos, sin, q_g, k_g, ones_bd, tm_pref=512):
    b, t, _ = proj.shape
    tm = _row_tile(t, tm_pref)
    kcol = 3 * ATTN_WIDTH // KV_WIDTH
    return pl.pallas_call(
        _qk_prep_kernel,
        grid=(b, t // tm),
        in_specs=[
            pl.BlockSpec((1, tm, ATTN_WIDTH), lambda bi, i: (bi, i, 0)),
            pl.BlockSpec((1, tm, KV_WIDTH), lambda bi, i: (bi, i, kcol)),
            pl.BlockSpec((1, tm, KV_WIDTH), lambda bi, i: (bi, i, kcol + 1)),
            pl.BlockSpec((tm, ATTN_WIDTH), lambda bi, i: (i, 0)),
            pl.BlockSpec((tm, ATTN_WIDTH), lambda bi, i: (i, 0)),
            pl.BlockSpec((1, ATTN_WIDTH), lambda bi, i: (0, 0)),
            pl.BlockSpec((1, KV_WIDTH), lambda bi, i: (0, 0)),
            pl.BlockSpec((V7X_MXU_DIM, V7X_MXU_DIM), lambda bi, i: (0, 0)),
        ],
        out_specs=[
            pl.BlockSpec((1, tm, ATTN_WIDTH), lambda bi, i: (bi, i, 0)),
            pl.BlockSpec((1, tm, KV_WIDTH), lambda bi, i: (bi, i, 0)),
            pl.BlockSpec((1, tm, KV_WIDTH), lambda bi, i: (bi, i, 0)),
        ],
        out_shape=[
            jax.ShapeDtypeStruct((b, t, ATTN_WIDTH), BF16),
            jax.ShapeDtypeStruct((b, t, KV_WIDTH), BF16),
            jax.ShapeDtypeStruct((b, t, KV_WIDTH), BF16),
        ],
        compiler_params=_cparams("parallel", "parallel"),
        name="qk_prep",
    )(proj, proj, proj, cos, sin, jnp.tile(q_g, ATTN_Q_HEADS).reshape(1, ATTN_WIDTH),
      jnp.tile(k_g, ATTN_KV_HEADS).reshape(1, KV_WIDTH), ones_bd)


def _attn_kernel(q_ref, k_ref, v_ref, o_ref):
    for g in range(ATTN_KV_HEADS):
        kg = k_ref[0, :, g * HEAD_DIM:(g + 1) * HEAD_DIM]
        vg = v_ref[0, :, g * HEAD_DIM:(g + 1) * HEAD_DIM]
        for hh in range(ATTN_GROUP):
            lo = (g * ATTN_GROUP + hh) * HEAD_DIM
            s = _dot_nt(q_ref[0, :, lo:lo + HEAD_DIM], kg)
            m = jnp.max(s, axis=-1, keepdims=True)
            p = jnp.exp(s - m)
            l = jnp.sum(p, axis=-1, keepdims=True)
            o = _dot(p.astype(BF16), vg) / l
            o_ref[0, :, lo:lo + HEAD_DIM] = o.astype(o_ref.dtype)


def _attention(q, k, v, tq_pref=256):
    b, t, _ = q.shape
    tk = k.shape[1]
    tq = _row_tile(t, tq_pref)
    return pl.pallas_call(
        _attn_kernel,
        grid=(b, t // tq),
        in_specs=[
            pl.BlockSpec((1, tq, ATTN_WIDTH), lambda bi, i: (bi, i, 0)),
            pl.BlockSpec((1, tk, KV_WIDTH), lambda bi, i: (bi, 0, 0)),
            pl.BlockSpec((1, tk, KV_WIDTH), lambda bi, i: (bi, 0, 0)),
        ],
        out_specs=pl.BlockSpec((1, tq, ATTN_WIDTH), lambda bi, i: (bi, i, 0)),
        out_shape=jax.ShapeDtypeStruct((b, t, ATTN_WIDTH), BF16),
        compiler_params=_cparams("parallel", "parallel"),
        name="attention",
    )(q, k, v)


def _conformer_kernel(a_ref, g_ref, ap_ref, gp_ref, an_ref, gn_ref, dw_ref, db_ref, lg_ref, lb_ref,
                      o_ref, u_ref, *, tm):
    i = pl.program_id(1)
    nt = pl.num_programs(1)
    u_ref[0:HALO] = jnp.where(i == 0, 0.0, ap_ref[0] * jax.nn.sigmoid(gp_ref[0]))
    u_ref[HALO:HALO + tm] = a_ref[0] * jax.nn.sigmoid(g_ref[0])
    u_ref[HALO + tm:2 * HALO + tm] = jnp.where(i == nt - 1, 0.0, an_ref[0] * jax.nn.sigmoid(gn_ref[0]))
    half = CONV_KERNEL // 2
    acc = jnp.zeros((tm, u_ref.shape[1]), F32) + db_ref[...]
    for j in range(CONV_KERNEL):
        acc = acc + u_ref[pl.ds(HALO - half + j, tm), :] * dw_ref[j:j + 1, :]
    mean = jnp.mean(acc, axis=-1, keepdims=True)
    cen = acc - mean
    var = jnp.mean(cen * cen, axis=-1, keepdims=True)
    y = cen * lax.rsqrt(var + LN_EPS) * lg_ref[...] + lb_ref[...]
    o_ref[0] = _silu(y).astype(o_ref.dtype)


def _conformer_conv(proj, dw, dw_b, ln_g, ln_b, tm_pref=256):
    b, t, _ = proj.shape
    cw = dw.shape[1]
    tm = _row_tile(t, tm_pref)
    hb = tm // HALO
    nhb = t // HALO
    dwp = jnp.zeros((32, cw), F32).at[:CONV_KERNEL].set(dw)
    prev = lambda c: (lambda bi, i: (bi, jnp.maximum(i * hb - 1, 0), c))
    nxt = lambda c: (lambda bi, i: (bi, jnp.minimum((i + 1) * hb, nhb - 1), c))
    vec = pl.BlockSpec((1, cw), lambda bi, i: (0, 0))
    return pl.pallas_call(
        functools.partial(_conformer_kernel, tm=tm),
        grid=(b, t // tm),
        in_specs=[
            pl.BlockSpec((1, tm, cw), lambda bi, i: (bi, i, 1)),
            pl.BlockSpec((1, tm, cw), lambda bi, i: (bi, i, 2)),
            pl.BlockSpec((1, HALO, cw), prev(1)),
            pl.BlockSpec((1, HALO, cw), prev(2)),
            pl.BlockSpec((1, HALO, cw), nxt(1)),
            pl.BlockSpec((1, HALO, cw), nxt(2)),
            pl.BlockSpec((32, cw), lambda bi, i: (0, 0)),
            vec, vec, vec,
        ],
        out_specs=pl.BlockSpec((1, tm, cw), lambda bi, i: (bi, i, 0)),
        out_shape=jax.ShapeDtypeStruct((b, t, cw), BF16),
        scratch_shapes=[pltpu.VMEM((tm + 2 * HALO, cw), F32)],
        compiler_params=_cparams("parallel", "parallel"),
        name="conformer_conv",
    )(proj, proj, proj, proj, proj, proj, dwp, dw_b.reshape(1, cw), ln_g.reshape(1, cw), ln_b.reshape(1, cw))


def _hyb_out_kernel(x_ref, a_ref, c_ref, gt_ref, wa_ref, wc_ref, o_ref):
    y = _dot(a_ref[0], wa_ref[...]) + _dot(c_ref[0], wc_ref[...])
    o_ref[0] = x_ref[0] + gt_ref[0] * y


def _hybrid_out(x, attn, conv, gate, w_out, tm_pref=512):
    b, t, d = x.shape
    tm = _row_tile(t, tm_pref)
    aw = attn.shape[-1]
    cw = conv.shape[-1]
    return pl.pallas_call(
        _hyb_out_kernel,
        grid=(b, t // tm),
        in_specs=[
            pl.BlockSpec((1, tm, d), lambda bi, i: (bi, i, 0)),
            pl.BlockSpec((1, tm, aw), lambda bi, i: (bi, i, 0)),
            pl.BlockSpec((1, tm, cw), lambda bi, i: (bi, i, 0)),
            pl.BlockSpec((1, 1, d), lambda bi, i: (bi, 0, 0)),
            pl.BlockSpec((aw, d), lambda bi, i: (0, 0)),
            pl.BlockSpec((cw, d), lambda bi, i: (0, 0)),
        ],
        out_specs=pl.BlockSpec((1, tm, d), lambda bi, i: (bi, i, 0)),
        out_shape=jax.ShapeDtypeStruct((b, t, d), F32),
        compiler_params=_cparams("parallel", "parallel"),
        name="hybrid_out",
    )(x, attn, conv, gate, w_out[:aw], w_out[aw:])


def _rwkv_feat_kernel(x_ref, xp_ref, xn_ref, g_ref, sh_ref, sc_ref, mu_ref, wr_ref, wk_ref, wv_ref,
                      g1_ref, g2_ref, w1_ref, w2f_ref, w2r_ref, a1_ref, a2f_ref, a2r_ref,
                      w0_ref, a0_ref, kk_ref, ka_ref, u_ref, ones_ref,
                      r_o, v_o, kkn_o, gg_o, bonus_o, lwf_o, kdf_o, bf_o, lwr_o, kdr_o, br_o, *, tm):
    i = pl.program_id(1)
    nt = pl.num_programs(1)
    g, sh, sc = g_ref[...], sh_ref[0], sc_ref[0]
    h = _norm_mod(x_ref[0], g, sh, sc)
    hp = jnp.where(i == 0, 0.0, _norm_mod(xp_ref[0, HALO - 1:HALO, :], g, sh, sc))
    hn = jnp.where(i == nt - 1, 0.0, _norm_mod(xn_ref[0, 0:1, :], g, sh, sc))
    rows = lax.broadcasted_iota(jnp.int32, (tm, 1), 0)
    up = jnp.where(rows == 0, hp, pltpu.roll(h, 1, 0))
    dn = jnp.where(rows == tm - 1, hn, pltpu.roll(h, tm - 1, 0))
    xx = 0.5 * (up + dn) - h

    def mix(n):
        return (h + xx * mu_ref[n:n + 1, :]).astype(BF16)

    r = _dot(mix(0), wr_ref[...])
    k = _dot(mix(2), wk_ref[...])
    v = _dot(mix(3), wv_ref[...])
    gg = _dot(jax.nn.sigmoid(_dot(mix(5), g1_ref[...])).astype(BF16), g2_ref[...])
    tl = jnp.tanh(_dot(mix(1), w1_ref[...])).astype(BF16)
    al = _dot(mix(4), a1_ref[...]).astype(BF16)

    ones_bd = ones_ref[...]
    kkf = k * kk_ref[...]
    kkn = kkf / jnp.maximum(jnp.sqrt(_head_sum(kkf * kkf, ones_bd)), 1e-12)
    r_o[0] = r
    v_o[0] = v
    kkn_o[0] = kkn
    gg_o[0] = gg

    bonus = jnp.zeros_like(r)
    outs = ((w2f_ref, a2f_ref, lwf_o, kdf_o, bf_o), (w2r_ref, a2r_ref, lwr_o, kdr_o, br_o))
    for dd, (w2_ref, a2_ref, lw_o, kd_o, b_o) in enumerate(outs):
        z = -(w0_ref[dd:dd + 1, :] + _dot(tl, w2_ref[...]))
        softplus = jnp.maximum(z, 0.0) + jnp.log1p(jnp.exp(-jnp.abs(z)))
        lw_o[0] = -jnp.exp(-softplus - 0.5)
        a = jax.nn.sigmoid(a0_ref[dd:dd + 1, :] + _dot(al, a2_ref[...]))
        kd = k * (1.0 + (a - 1.0) * ka_ref[...])
        kd_o[0] = kd
        b_o[0] = kkn * a
        bonus = bonus + _head_sum(r * kd * u_ref[dd:dd + 1, :], ones_bd) * v
    bonus_o[0] = bonus


def _rwkv_features(x, g, shift, scale, p, ones_bd, tm_pref=256):
    b, t, d = x.shape
    tm = _row_tile(t, tm_pref)
    hb = tm // HALO
    nhb = t // HALO
    row = pl.BlockSpec((1, tm, d), lambda bi, i: (bi, i, 0))
    vec3 = pl.BlockSpec((1, 1, d), lambda bi, i: (bi, 0, 0))

    def full(a):
        return pl.BlockSpec(a.shape, lambda bi, i: (0,) * a.ndim)

    consts = [p["mu"], p["wr"], p["wk"], p["wv"], p["g1"], p["g2"], p["w1"], p["w2f"], p["w2r"],
              p["a1"], p["a2f"], p["a2r"], p["w0"], p["a0"], p["kk"], p["ka"], p["u"], ones_bd]
    n_out = 11
    return pl.pallas_call(
        functools.partial(_rwkv_feat_kernel, tm=tm),
        grid=(b, t // tm),
        in_specs=[
            row,
            pl.BlockSpec((1, HALO, d), lambda bi, i: (bi, jnp.maximum(i * hb - 1, 0), 0)),
            pl.BlockSpec((1, HALO, d), lambda bi, i: (bi, jnp.minimum((i + 1) * hb, nhb - 1), 0)),
            pl.BlockSpec((1, d), lambda bi, i: (0, 0)),
            vec3, vec3,
        ] + [full(a) for a in consts],
        out_specs=[row] * n_out,
        out_shape=[jax.ShapeDtypeStruct((b, t, d), F32)] * n_out,
        compiler_params=_cparams("parallel", "parallel"),
        name="rwkv_features",
    )(x, x, x, g.reshape(1, d), shift, scale, *consts)


def _split_bf16(x, n):
    parts, rem = [], x
    for _ in range(n):
        p = rem.astype(BF16)
        parts.append(p)
        rem = rem - p.astype(F32)
    return parts


def _mm_parts(a_parts, b_parts, fn):
    n = max(len(a_parts), len(b_parts))
    acc = None
    for i in reversed(range(len(a_parts))):
        for j in reversed(range(len(b_parts))):
            if i + j < n:
                term = fn(a_parts[i], b_parts[j])
                acc = term if acc is None else acc + term
    return acc


def _bdot(a, b):
    return lax.dot_general(a, b, (((2,), (1,)), ((0,), (0,))), preferred_element_type=F32)


def _bdot_nt(a, b):
    return lax.dot_general(a, b, (((2,), (2,)), ((0,), (0,))), preferred_element_type=F32)


def _bdot_tn(a, b):
    return lax.dot_general(a, b, (((1,), (1,)), ((0,), (0,))), preferred_element_type=F32)


def _mm_stacked(a_parts, b_parts, fn, rows):
    if len(a_parts) != 2 or len(b_parts) != 2:
        return _mm_parts(a_parts, b_parts, fn)
    both = fn(jnp.concatenate(a_parts, axis=1), b_parts[0])
    return fn(a_parts[0], b_parts[1]) + both[:, rows:] + both[:, :rows]


def _scan_kernel(rf_ref, vf_ref, kkf_ref, rr_ref, vr_ref, kkr_ref, lwf_ref, kdf_ref, bf_ref,
                 lwr_ref, kdr_ref, br_ref, s0_ref, ms_ref, mi_ref, lvl_ref, bd_ref, eye_ref,
                 yf_ref, yr_ref, sf_ref, st_ref, *, passes):
    c = pl.program_id(1)
    nc = pl.num_programs(1)
    L = rf_ref.shape[1]
    W = V7X_MXU_DIM
    G = rf_ref.shape[2] // W

    @pl.when(c == 0)
    def _():
        st_ref[...] = s0_ref[0]

    mask_strict = ms_ref[...]
    mask_incl = mi_ref[...]
    bd = bd_ref[...][None]
    bd16 = bd.astype(BF16)
    eye = eye_ref[...]
    eye_row = jnp.concatenate([eye[:L, :L]] * SCAN_GROUP, axis=1)[None]

    def groups(a):
        return jnp.stack([a[:, g * W:(g + 1) * W] for g in range(G)], axis=0)

    def expand(parts):
        return [jnp.concatenate([p] * SCAN_GROUP, axis=1) * bd16 for p in parts]

    def cat(parts_a, parts_b):
        return [jnp.concatenate([a, b], axis=1) for a, b in zip(parts_a, parts_b)]

    def prep(r_ref, v_ref, kk_ref, lw_ref, kd_ref, b_ref, tri, last):
        lw = lw_ref[0]
        cs = _mm_parts([tri.astype(BF16)], _split_bf16(lw, passes["cs"]), _dot)
        gam = jnp.exp(cs)
        gam_inv = jnp.exp(-cs)
        gam_end = gam[last:last + 1, :]
        kt = kd_ref[0] * gam_inv
        bt = b_ref[0] * gam_inv
        return dict(kkg=groups(kk_ref[0] * jnp.exp(cs - lw)), rg=groups(r_ref[0] * gam), kt=groups(kt),
                    bt=groups(bt), kb=groups(jnp.concatenate([kt * gam_end, -(bt * gam_end)], axis=0)),
                    v=groups(v_ref[0]), gend=groups(gam_end))

    fw = prep(rf_ref, vf_ref, kkf_ref, lwf_ref, kdf_ref, bf_ref, mi_ref[0, :, :L], L - 1)
    rv = prep(rr_ref, vr_ref, kkr_ref, lwr_ref, kdr_ref, br_ref, mi_ref[G, :, :L], 0)
    both_dirs = {k: jnp.concatenate([fw[k], rv[k]], axis=0) for k in fw}
    kkg, rg, kt, bt, kb, v, gend = (both_dirs[k] for k in ("kkg", "rg", "kt", "bt", "kb", "v", "gend"))
    st = st_ref[...]

    pa, pb = passes["sc"]
    lhs = jnp.concatenate([kkg, rg], axis=1)
    lhs_p = _split_bf16(lhs, max(pa, passes["sprod"][0]))
    rhs_p = cat(expand(_split_bf16(bt, pb)), expand(_split_bf16(kt, pb)))
    sc = _mm_parts(lhs_p[:pa], rhs_p, _bdot_nt)
    m_row = sc[:, :L, :W] * mask_strict
    n_row = sc[:, :L, W:] * mask_strict
    ab_row = sc[:, L:, :W] * mask_incl
    ak_row = sc[:, L:, W:] * mask_incl

    pa, pb = passes["inv"]
    t_row = eye_row - m_row * lvl_ref[0][None]
    for lv in range(1, lvl_ref.shape[0]):
        off_p = expand(_split_bf16(m_row * lvl_ref[lv][None], pb))
        x = _mm_stacked(_split_bf16(t_row, pa), off_p, _bdot, L)
        t_row = t_row - _mm_stacked(_split_bf16(x, pa), expand(_split_bf16(t_row, pb)), _bdot, L)

    pa, pb = passes["sprod"]
    s_prod = _mm_parts(lhs_p[:pa], _split_bf16(st, pb), _bdot)
    pa, pb = passes["nv"]
    v_p = _split_bf16(v, max(pb, passes["upd"][1]))
    nv = _mm_parts(_split_bf16(jnp.concatenate([n_row, ak_row], axis=1), pa), expand(v_p[:pb]), _bdot)
    pa, pb = passes["u"]
    u = _mm_parts(_split_bf16(t_row, pa), expand(_split_bf16(s_prod[:, :L] + nv[:, :L], pb)), _bdot)
    pa, pb = passes["abu"]
    u_p = _split_bf16(u, max(pb, passes["upd"][1]))
    y = s_prod[:, L:] + nv[:, L:] - _mm_parts(_split_bf16(ab_row, pa), expand(u_p[:pb]), _bdot)
    for g in range(G):
        yf_ref[0, :, g * W:(g + 1) * W] = y[g]
        yr_ref[0, :, g * W:(g + 1) * W] = y[G + g]

    pa, pb = passes["upd"]
    upd = _mm_parts(_split_bf16(kb, pa), cat(v_p[:pb], u_p[:pb]), _bdot_tn) * bd
    gcol = jnp.sum(eye[None] * gend, axis=2, keepdims=True)
    st_ref[...] = st * gcol + upd

    @pl.when(c == nc - 1)
    def _():
        sf_ref[0] = st_ref[...]


def _scan_masks(L, groups):
    t = jnp.arange(L)[:, None]
    i = jnp.arange(L)[None, :]
    tile = lambda m: jnp.broadcast_to(jnp.tile(m.astype(F32), (1, SCAN_GROUP))[None], (groups, L, SCAN_GROUP * L))
    strict = jnp.concatenate([tile(i < t), tile(i > t)], axis=0)
    incl = jnp.concatenate([tile(i <= t), tile(i >= t)], axis=0)
    sizes = [2 ** k for k in range(L.bit_length() - 1)]
    levels = jnp.stack([jnp.tile(((t // (2 * s) == i // (2 * s)) & (t // s != i // s)).astype(F32),
                                 (1, SCAN_GROUP)) for s in sizes], axis=0)
    return strict, incl, levels


def _wkv_scan(r, v, kk, lw_f, kd_f, b_f, lw_r, kd_r, b_r, s0):
    b, t, d = r.shape
    L = SCAN_CHUNK
    nc = t // L
    W = V7X_MXU_DIM
    ng = d // W
    ms, mi, lvl = _scan_masks(L, ng)
    hid = jnp.arange(W) // HEAD_DIM
    bd = (hid[:, None] == hid[None, :]).astype(F32)
    eye = jnp.eye(W, dtype=F32)
    fwd = pl.BlockSpec((1, L, d), lambda bi, c: (bi, c, 0))
    rev = pl.BlockSpec((1, L, d), lambda bi, c: (bi, nc - 1 - c, 0))
    state = pl.BlockSpec((1, 2 * ng, W, W), lambda bi, c: (bi, 0, 0, 0))
    const = lambda a: pl.BlockSpec(a.shape, lambda bi, c: (0,) * a.ndim)
    return pl.pallas_call(
        functools.partial(_scan_kernel, passes=SCAN_PASSES),
        grid=(b, nc),
        in_specs=[fwd] * 3 + [rev] * 3 + [fwd] * 3 + [rev] * 3 + [state, const(ms), const(mi), const(lvl), const(bd), const(eye)],
        out_specs=[fwd, rev, state],
        out_shape=[jax.ShapeDtypeStruct((b, t, d), F32), jax.ShapeDtypeStruct((b, t, d), F32),
                   jax.ShapeDtypeStruct((b, 2 * ng, W, W), F32)],
        scratch_shapes=[pltpu.VMEM((2 * ng, W, W), F32)],
        compiler_params=_cparams("parallel", "arbitrary"),
        name="wkv_scan",
    )(r, v, kk, r, v, kk, lw_f, kd_f, b_f, lw_r, kd_r, b_r, s0, ms, mi, lvl, bd, eye)


def _rwkv_out_kernel(x_ref, yf_ref, yr_ref, bonus_ref, gg_ref, gt_ref, lg_ref, lb_ref, wo_ref, ones_ref, o_ref):
    ones_bd = ones_ref[...]
    inv = 1.0 / HEAD_DIM
    y = yf_ref[0] + yr_ref[0]
    cen = y - _head_sum(y, ones_bd) * inv
    var = _head_sum(cen * cen, ones_bd) * inv
    yn = cen * lax.rsqrt(var + LNX_EPS) * lg_ref[...] + lb_ref[...]
    z = ((yn + bonus_ref[0]) * gg_ref[0]).astype(BF16)
    o_ref[0] = x_ref[0] + gt_ref[0] * _dot(z, wo_ref[...])


def _rwkv_out(x, yf, yr, bonus, gg, gate, lnx_g, lnx_b, wo, ones_bd, tm_pref=256):
    b, t, d = x.shape
    tm = _row_tile(t, tm_pref)
    row = pl.BlockSpec((1, tm, d), lambda bi, i: (bi, i, 0))
    vec = pl.BlockSpec((1, d), lambda bi, i: (0, 0))
    return pl.pallas_call(
        _rwkv_out_kernel,
        grid=(b, t // tm),
        in_specs=[row] * 5 + [
            pl.BlockSpec((1, 1, d), lambda bi, i: (bi, 0, 0)), vec, vec,
            pl.BlockSpec((d, d), lambda bi, i: (0, 0)),
            pl.BlockSpec((V7X_MXU_DIM, V7X_MXU_DIM), lambda bi, i: (0, 0)),
        ],
        out_specs=row,
        out_shape=jax.ShapeDtypeStruct((b, t, d), F32),
        compiler_params=_cparams("parallel", "parallel"),
        name="rwkv_out",
    )(x, yf, yr, bonus, gg, gate, lnx_g.reshape(1, d), lnx_b.reshape(1, d), wo, ones_bd)


def _final_norm_kernel(x_ref, g_ref, o_ref):
    x = x_ref[0]
    ms = jnp.mean(x * x, axis=-1, keepdims=True)
    o_ref[0] = x * lax.rsqrt(ms + NORM_EPS) * g_ref[...]


def _final_norm(x, g, tm_pref=512):
    b, t, d = x.shape
    tm = _row_tile(t, tm_pref)
    return pl.pallas_call(
        _final_norm_kernel,
        grid=(b, t // tm),
        in_specs=[pl.BlockSpec((1, tm, d), lambda bi, i: (bi, i, 0)), pl.BlockSpec((1, d), lambda bi, i: (0, 0))],
        out_specs=pl.BlockSpec((1, tm, d), lambda bi, i: (bi, i, 0)),
        out_shape=jax.ShapeDtypeStruct((b, t, d), F32),
        compiler_params=_cparams("parallel", "parallel"),
        name="final_norm",
    )(x, g.reshape(1, d))


def _rope_tables(t):
    n_freq = HEAD_DIM // 4
    inv = ROPE_THETA ** (-jnp.arange(n_freq, dtype=F32) / n_freq)
    pos = jnp.arange(t, dtype=jnp.int32)
    row = (pos // GRID_W).astype(F32)[:, None] * inv
    col = (pos % GRID_W).astype(F32)[:, None] * inv
    cos = jnp.concatenate([jnp.cos(row)] * 2 + [jnp.cos(col)] * 2, axis=1)
    sin = jnp.concatenate([-jnp.sin(row), jnp.sin(row), -jnp.sin(col), jnp.sin(col)], axis=1)
    return jnp.tile(cos, (1, ATTN_Q_HEADS)), jnp.tile(sin, (1, ATTN_Q_HEADS))


def _hybrid_layer(xc, xl, mc, ml, norm_g, w_in, q_g, k_g, dw, dw_b, ln_g, ln_b, w_out, tables, ones_bd):
    d = xl.shape[-1]
    o1, o2, o3 = ATTN_WIDTH, ATTN_WIDTH + KV_WIDTH, ATTN_WIDTH + 2 * KV_WIDTH
    w_perm = jnp.concatenate([w_in[:, :o1], w_in[:, o3:], w_in[:, o1:o3]], axis=1).astype(BF16)
    w_out16 = w_out.astype(BF16)
    (cos_l, sin_l), (cos_c, sin_c) = tables
    pl_ = _norm_mod_matmul(xl, norm_g, ml[0], ml[1], w_perm)
    pc_ = _norm_mod_matmul(xc, norm_g, mc[0], mc[1], w_perm)
    ql, kl, vl = _qk_prep(pl_, cos_l, sin_l, q_g, k_g, ones_bd)
    qc, kc, vc = _qk_prep(pc_, cos_c, sin_c, q_g, k_g, ones_bd)
    k_all = jnp.concatenate([kc, kl], axis=1)
    v_all = jnp.concatenate([vc, vl], axis=1)
    attn_l = _attention(ql, k_all, v_all)
    attn_c = _attention(qc, kc, vc)
    conv_l = _conformer_conv(pl_, dw, dw_b, ln_g, ln_b)
    conv_c = _conformer_conv(pc_, dw, dw_b, ln_g, ln_b)
    xl = _hybrid_out(xl, attn_l, conv_l, ml[2], w_out16)
    xc = _hybrid_out(xc, attn_c, conv_c, mc[2], w_out16)
    return xc, xl


def _rwkv_layer(xc, xl, mc, ml, norm_g, mu, wr, wk, wv, wo, w0, w1, w2, a0, a1, a2, g1, g2,
                k_k, k_a, u, lnx_g, lnx_b, ones_bd, ctx_out):
    b, _, d = xl.shape
    lora_w = w1.shape[-1]
    zeros_w = jnp.zeros((lora_w, d), F32)
    p = {
        "mu": jnp.zeros((V7X_SUBLANES, d), F32).at[:6].set(mu),
        "wr": wr.astype(BF16), "wk": wk.astype(BF16), "wv": wv.astype(BF16),
        "g1": g1.astype(BF16), "g2": g2.astype(BF16),
        "w1": jnp.concatenate([w1[0], w1[1]], axis=1).astype(BF16),
        "w2f": jnp.concatenate([w2[0], zeros_w], axis=0).astype(BF16),
        "w2r": jnp.concatenate([zeros_w, w2[1]], axis=0).astype(BF16),
        "a1": jnp.concatenate([a1[0], a1[1]], axis=1).astype(BF16),
        "a2f": jnp.concatenate([a2[0], jnp.zeros_like(a2[1])], axis=0).astype(BF16),
        "a2r": jnp.concatenate([jnp.zeros_like(a2[0]), a2[1]], axis=0).astype(BF16),
        "w0": jnp.zeros((V7X_SUBLANES, d), F32).at[:2].set(w0),
        "a0": jnp.zeros((V7X_SUBLANES, d), F32).at[:2].set(a0),
        "kk": k_k.reshape(1, d), "ka": k_a.reshape(1, d),
        "u": jnp.zeros((V7X_SUBLANES, d), F32).at[:2].set(u.reshape(2, d)),
    }
    fc = _rwkv_features(xc, norm_g, mc[0], mc[1], p, ones_bd)
    fl = _rwkv_features(xl, norm_g, ml[0], ml[1], p, ones_bd)
    ng = d // V7X_MXU_DIM
    state = jnp.zeros((b, 2 * ng, V7X_MXU_DIM, V7X_MXU_DIM), F32)
    ys = {}
    for name, f in (("c", fc), ("l", fl)):
        r_, v_, kkn = f[:3]
        y_f, y_r, state = _wkv_scan(r_, v_, kkn, f[5], f[6], f[7], f[8], f[9], f[10], state)
        ys[(name, 0)], ys[(name, 1)] = y_f, y_r
    wo16 = wo.astype(BF16)
    xl = _rwkv_out(xl, ys[("l", 0)], ys[("l", 1)], fl[4], fl[3], ml[2], lnx_g, lnx_b, wo16, ones_bd)
    if ctx_out:
        xc = _rwkv_out(xc, ys[("c", 0)], ys[("c", 1)], fc[4], fc[3], mc[2], lnx_g, lnx_b, wo16, ones_bd)
    return xc, xl


def kernel(x, c, ctx, c_ctx, mod_w, mod_b, norm_mix, norm_ffn, ffn_w_in, ffn_dw, ffn_dw_b, ffn_w_out, hyb_w_in, hyb_q_norm, hyb_k_norm, hyb_dw, hyb_dw_b, hyb_ln_g, hyb_ln_b, hyb_w_out, rwkv_mu, rwkv_wr, rwkv_wk, rwkv_wv, rwkv_wo, rwkv_w0, rwkv_w1, rwkv_w2, rwkv_a0, rwkv_a1, rwkv_a2, rwkv_g1, rwkv_g2, rwkv_kk, rwkv_ka, rwkv_u, rwkv_lnx_g, rwkv_lnx_b, final_norm):
    b, t, d = x.shape
    n_ctx = ctx.shape[1]
    depth = mod_w.shape[0]
    assert d % V7X_MXU_DIM == 0 and t % SCAN_CHUNK == 0 and n_ctx % SCAN_CHUNK == 0
    assert t % GRID_W == 0 and t % HALO == 0 and n_ctx % HALO == 0

    m_rows = -(-(b + 1) // V7X_SUBLANES) * V7X_SUBLANES
    cvec = jnp.zeros((m_rows, d), F32).at[:b].set(c).at[b].set(c_ctx)
    mod = _modulation(cvec, mod_w, mod_b)
    mod = mod.reshape(depth, m_rows, 6, d)

    hid = jnp.arange(V7X_MXU_DIM) // HEAD_DIM
    ones_bd = (hid[:, None] == hid[None, :]).astype(BF16)
    cos_l, sin_l = _rope_tables(t)
    tables = ((cos_l, sin_l), (jnp.ones((n_ctx, ATTN_WIDTH), F32), jnp.zeros((n_ctx, ATTN_WIDTH), F32)))

    xl, xc = x, ctx
    for i in range(depth):
        last = i == depth - 1
        j = i // 2
        ml = [mod[i, :b, n][:, None, :] for n in range(6)]
        mc = [jnp.broadcast_to(mod[i, b, n][None, None, :], (b, 1, d)) for n in range(6)]
        if i % 2 == 0:
            xc, xl = _hybrid_layer(xc, xl, mc, ml, norm_mix[i], hyb_w_in[j], hyb_q_norm[j], hyb_k_norm[j],
                                   hyb_dw[j], hyb_dw_b[j], hyb_ln_g[j], hyb_ln_b[j], hyb_w_out[j],
                                   tables, ones_bd)
        else:
            xc, xl = _rwkv_layer(xc, xl, mc, ml, norm_mix[i], rwkv_mu[j], rwkv_wr[j], rwkv_wk[j], rwkv_wv[j],
                                 rwkv_wo[j], rwkv_w0[j], rwkv_w1[j], rwkv_w2[j], rwkv_a0[j], rwkv_a1[j],
                                 rwkv_a2[j], rwkv_g1[j], rwkv_g2[j], rwkv_kk[j], rwkv_ka[j], rwkv_u[j],
                                 rwkv_lnx_g[j], rwkv_lnx_b[j], ones_bd, not last)
        w_in16 = ffn_w_in[i].astype(BF16)
        w_out16 = ffn_w_out[i].astype(BF16)
        xl = _conv_ffn(xl, norm_ffn[i], ml[3], ml[4], ml[5], w_in16, ffn_dw[i], ffn_dw_b[i], w_out16)
        if not last:
            xc = _conv_ffn(xc, norm_ffn[i], mc[3], mc[4], mc[5], w_in16, ffn_dw[i], ffn_dw_b[i], w_out16)
    return _final_norm(xl, final_norm)
```

```python
import functools

import jax
import jax.numpy as jnp
from jax import lax
from jax.experimental import pallas as pl
from jax.experimental.pallas import tpu as pltpu

F32 = jnp.float32
BF16 = jnp.bfloat16
HIGHEST = lax.Precision.HIGHEST

HEAD_DIM = 64
ATTN_Q_HEADS = 8
ATTN_KV_HEADS = 2
ATTN_GROUP = ATTN_Q_HEADS // ATTN_KV_HEADS
ATTN_WIDTH = ATTN_Q_HEADS * HEAD_DIM
KV_WIDTH = ATTN_KV_HEADS * HEAD_DIM
CONV_KERNEL = 31
GRID_W = 64
ROPE_THETA = 10000.0
NORM_EPS = 1e-6
LN_EPS = 1e-5
LNX_EPS = 64e-5
LOG2_E = 1.4426950408889634

V7X_LANES = 128
V7X_SUBLANES = 8
V7X_MXU_DIM = 256
V7X_VMEM_BYTES = 64 * 1024 * 1024
VMEM_LIMIT_BYTES = 58 * 1024 * 1024

HALO = 16
SCAN_CHUNK = 64
SCAN_GROUP = V7X_MXU_DIM // HEAD_DIM
SCAN_PASSES = {"cs": 2, "sc": (1, 1), "inv": (1, 1), "sprod": (1, 1), "nv": (1, 1), "u": (1, 1),
               "abu": (1, 1), "upd": (1, 1)}


def _cparams(*sem):
    return pltpu.CompilerParams(dimension_semantics=sem, vmem_limit_bytes=VMEM_LIMIT_BYTES)


def _row_tile(t, pref):
    return pref if t % pref == 0 else t


def _dot(a, b, precision=None):
    return jnp.dot(a, b, preferred_element_type=F32, precision=precision)


def _dot_nt(a, b, precision=None):
    return lax.dot_general(a, b, (((1,), (1,)), ((), ())), preferred_element_type=F32, precision=precision)


def _dot_tn(a, b, precision=None):
    return lax.dot_general(a, b, (((0,), (0,)), ((), ())), preferred_element_type=F32, precision=precision)


def _silu(x):
    return x * jax.nn.sigmoid(x)


def _norm_mod(x, g, shift, scale):
    ms = jnp.mean(x * x, axis=-1, keepdims=True)
    return (x * lax.rsqrt(ms + NORM_EPS) * g) * (1.0 + scale) + shift


def _head_sum(x, ones_bd):
    w = x.shape[-1]
    hi = x.astype(BF16)
    lo = (x - hi.astype(F32)).astype(BF16)
    outs = []
    for s in range(0, w, V7X_MXU_DIM):
        e = min(s + V7X_MXU_DIM, w)
        g = ones_bd[: e - s, : e - s]
        outs.append(_dot(hi[:, s:e], g) + _dot(lo[:, s:e], g))
    return outs[0] if len(outs) == 1 else jnp.concatenate(outs, axis=-1)


def _mod_kernel(c_ref, w_ref, b_ref, o_ref):
    s = _silu(c_ref[...])
    o_ref[0] = _dot(s, w_ref[0], HIGHEST) + b_ref[0]


def _modulation(cvec, mod_w, mod_b):
    depth, d, n = mod_w.shape
    m = cvec.shape[0]
    tn = 512
    return pl.pallas_call(
        _mod_kernel,
        grid=(depth, n // tn),
        in_specs=[
            pl.BlockSpec((m, d), lambda l, j: (0, 0)),
            pl.BlockSpec((1, d, tn), lambda l, j: (l, 0, j)),
            pl.BlockSpec((1, 1, tn), lambda l, j: (l, 0, j)),
        ],
        out_specs=pl.BlockSpec((1, m, tn), lambda l, j: (l, 0, j)),
        out_shape=jax.ShapeDtypeStruct((depth, m, n), F32),
        compiler_params=_cparams("parallel", "parallel"),
        name="modulation",
    )(cvec, mod_w, mod_b.reshape(depth, 1, n))


def _nmm_kernel(x_ref, g_ref, sh_ref, sc_ref, w_ref, o_ref):
    h = _norm_mod(x_ref[0], g_ref[...], sh_ref[0], sc_ref[0]).astype(BF16)
    o_ref[0] = _dot(h, w_ref[...]).astype(o_ref.dtype)


def _norm_mod_matmul(x, g, shift, scale, w, tm_pref=512, out_dtype=F32):
    b, t, d = x.shape
    n = w.shape[1]
    tm = _row_tile(t, tm_pref)
    return pl.pallas_call(
        _nmm_kernel,
        grid=(b, t // tm),
        in_specs=[
            pl.BlockSpec((1, tm, d), lambda bi, i: (bi, i, 0)),
            pl.BlockSpec((1, d), lambda bi, i: (0, 0)),
            pl.BlockSpec((1, 1, d), lambda bi, i: (bi, 0, 0)),
            pl.BlockSpec((1, 1, d), lambda bi, i: (bi, 0, 0)),
            pl.BlockSpec((d, n), lambda bi, i: (0, 0)),
        ],
        out_specs=pl.BlockSpec((1, tm, n), lambda bi, i: (bi, i, 0)),
        out_shape=jax.ShapeDtypeStruct((b, t, n), out_dtype),
        compiler_params=_cparams("parallel", "parallel"),
        name="norm_mod_matmul",
    )(x, g.reshape(1, d), shift, scale, w)


def _ffn_kernel(x_ref, xp_ref, xn_ref, g_ref, sh_ref, sc_ref, gt_ref, wg_ref, wv_ref,
                dw_ref, db_ref, wo_ref, fg_ref, o_ref, h_ref, gate_ref, acc_ref, *, tm, final_norm):
    i = pl.program_id(1)
    j = pl.program_id(2)
    nt = pl.num_programs(1)
    nf = pl.num_programs(2)

    @pl.when(j == 0)
    def _():
        g, sh, sc = g_ref[...], sh_ref[0], sc_ref[0]
        h_ref[0:HALO] = _norm_mod(xp_ref[0], g, sh, sc).astype(BF16)
        h_ref[HALO:HALO + tm] = _norm_mod(x_ref[0], g, sh, sc).astype(BF16)
        h_ref[HALO + tm:2 * HALO + tm] = _norm_mod(xn_ref[0], g, sh, sc).astype(BF16)
        acc_ref[...] = jnp.zeros_like(acc_ref)

    gate_ref[...] = _dot(h_ref[...], wg_ref[...])
    val = _dot(h_ref[HALO:HALO + tm], wv_ref[...])
    rows = lax.broadcasted_iota(jnp.int32, (tm, 1), 0)
    g_prev = jnp.where((rows == 0) & (i == 0), 0.0, gate_ref[pl.ds(HALO - 1, tm), :])
    g_next = jnp.where((rows == tm - 1) & (i == nt - 1), 0.0, gate_ref[pl.ds(HALO + 1, tm), :])
    conv = (g_prev * dw_ref[0:1, :] + gate_ref[pl.ds(HALO, tm), :] * dw_ref[1:2, :]
            + g_next * dw_ref[2:3, :] + db_ref[...])
    act = (_silu(conv) * val).astype(BF16)
    acc_ref[...] += _dot(act, wo_ref[...])

    @pl.when(j == nf - 1)
    def _():
        y = x_ref[0] + gt_ref[0] * acc_ref[...]
        if final_norm:
            ms = jnp.mean(y * y, axis=-1, keepdims=True)
            y = y * lax.rsqrt(ms + NORM_EPS) * fg_ref[...]
        o_ref[0] = y


def _conv_ffn(x, g, shift, scale, gate, w_in, dw, dw_b, w_out, final_g=None, tm_pref=512):
    b, t, d = x.shape
    f = w_out.shape[0]
    nf = 2
    fc = f // nf
    tm = _row_tile(t, tm_pref)
    hb = tm // HALO
    nhb = t // HALO
    dwp = jnp.zeros((V7X_SUBLANES, f), F32).at[:dw.shape[0]].set(dw)
    fg = jnp.ones((d,), F32) if final_g is None else final_g
    return pl.pallas_call(
        functools.partial(_ffn_kernel, tm=tm, final_norm=final_g is not None),
        grid=(b, t // tm, nf),
        in_specs=[
            pl.BlockSpec((1, tm, d), lambda bi, i, j: (bi, i, 0)),
            pl.BlockSpec((1, HALO, d), lambda bi, i, j: (bi, jnp.maximum(i * hb - 1, 0), 0)),
            pl.BlockSpec((1, HALO, d), lambda bi, i, j: (bi, jnp.minimum((i + 1) * hb, nhb - 1), 0)),
            pl.BlockSpec((1, d), lambda bi, i, j: (0, 0)),
            pl.BlockSpec((1, 1, d), lambda bi, i, j: (bi, 0, 0)),
            pl.BlockSpec((1, 1, d), lambda bi, i, j: (bi, 0, 0)),
            pl.BlockSpec((1, 1, d), lambda bi, i, j: (bi, 0, 0)),
            pl.BlockSpec((d, fc), lambda bi, i, j: (0, j)),
            pl.BlockSpec((d, fc), lambda bi, i, j: (0, nf + j)),
            pl.BlockSpec((V7X_SUBLANES, fc), lambda bi, i, j: (0, j)),
            pl.BlockSpec((1, fc), lambda bi, i, j: (0, j)),
            pl.BlockSpec((fc, d), lambda bi, i, j: (j, 0)),
            pl.BlockSpec((1, d), lambda bi, i, j: (0, 0)),
        ],
        out_specs=pl.BlockSpec((1, tm, d), lambda bi, i, j: (bi, i, 0)),
        out_shape=jax.ShapeDtypeStruct((b, t, d), F32),
        scratch_shapes=[
            pltpu.VMEM((tm + 2 * HALO, d), BF16),
            pltpu.VMEM((tm + 2 * HALO, fc), F32),
            pltpu.VMEM((tm, d), F32),
        ],
        compiler_params=_cparams("parallel", "parallel", "arbitrary"),
        name="conv_ffn",
    )(x, x, x, g.reshape(1, d), shift, scale, gate, w_in, w_in, dwp, dw_b.reshape(1, f), w_out, fg.reshape(1, d))


def _rope(x, cos, sin):
    w = x.shape[-1]
    lane = lax.broadcasted_iota(jnp.int32, x.shape, x.ndim - 1)
    partner = jnp.where(lane % 32 < 16, pltpu.roll(x, w - 16, x.ndim - 1), pltpu.roll(x, 16, x.ndim - 1))
    return x * cos + partner * sin


def _qk_prep_kernel(q_ref, k_ref, v_ref, cos_ref, sin_ref, qg_ref, kg_ref, ones_ref,
                    qo_ref, ko_ref, vo_ref):
    ones_bd = ones_ref[...]
    inv = 1.0 / HEAD_DIM
    q = q_ref[0]
    q = q * lax.rsqrt(_head_sum(q * q, ones_bd) * inv + NORM_EPS) * qg_ref[...]
    q = _rope(q, cos_ref[...], sin_ref[...])
    qo_ref[0] = (q * (HEAD_DIM ** -0.5 * LOG2_E)).T.astype(BF16)
    k = k_ref[0]
    k = k * lax.rsqrt(_head_sum(k * k, ones_bd) * inv + NORM_EPS) * kg_ref[...]
    k = _rope(k, cos_ref[:, :KV_WIDTH], sin_ref[:, :KV_WIDTH])
    for g in range(ATTN_KV_HEADS):
        ko_ref[0, g] = k[:, g * HEAD_DIM:(g + 1) * HEAD_DIM].astype(BF16)
    vo_ref[0] = v_ref[0].T.astype(BF16)


def _qk_prep(proj, cos, sin, q_g, k_g, ones_bd, tm_pref=512):
    b, t, _ = proj.shape
    tm = _row_tile(t, tm_pref)
    kcol = 3 * ATTN_WIDTH // KV_WIDTH
    return pl.pallas_call(
        _qk_prep_kernel,
        grid=(b, t // tm),
        in_specs=[
            pl.BlockSpec((1, tm, ATTN_WIDTH), lambda bi, i: (bi, i, 0)),
            pl.BlockSpec((1, tm, KV_WIDTH), lambda bi, i: (bi, i, kcol)),
            pl.BlockSpec((1, tm, KV_WIDTH), lambda bi, i: (bi, i, kcol + 1)),
            pl.BlockSpec((tm, ATTN_WIDTH), lambda bi, i: (i, 0)),
            pl.BlockSpec((tm, ATTN_WIDTH), lambda bi, i: (i, 0)),
            pl.BlockSpec((1, ATTN_WIDTH), lambda bi, i: (0, 0)),
            pl.BlockSpec((1, KV_WIDTH), lambda bi, i: (0, 0)),
            pl.BlockSpec((V7X_MXU_DIM, V7X_MXU_DIM), lambda bi, i: (0, 0)),
        ],
        out_specs=[
            pl.BlockSpec((1, ATTN_WIDTH, tm), lambda bi, i: (bi, 0, i)),
            pl.BlockSpec((1, ATTN_KV_HEADS, tm, HEAD_DIM), lambda bi, i: (bi, 0, i, 0)),
            pl.BlockSpec((1, KV_WIDTH, tm), lambda bi, i: (bi, 0, i)),
        ],
        out_shape=[
            jax.ShapeDtypeStruct((b, ATTN_WIDTH, t), BF16),
            jax.ShapeDtypeStruct((b, ATTN_KV_HEADS, t, HEAD_DIM), BF16),
            jax.ShapeDtypeStruct((b, KV_WIDTH, t), BF16),
        ],
        compiler_params=_cparams("parallel", "parallel"),
        name="qk_prep",
    )(proj, proj, proj, cos, sin, jnp.tile(q_g, ATTN_Q_HEADS).reshape(1, ATTN_WIDTH),
      jnp.tile(k_g, ATTN_KV_HEADS).reshape(1, KV_WIDTH), ones_bd)


def _attn_kernel(qt_ref, k_ref, vt_ref, ot_ref):
    tq = qt_ref.shape[2]
    for g in range(ATTN_KV_HEADS):
        heads = [g * ATTN_GROUP + hh for hh in range(ATTN_GROUP)]
        qg = jnp.concatenate([qt_ref[0, h * HEAD_DIM:(h + 1) * HEAD_DIM, :] for h in heads], axis=1)
        st = _dot(k_ref[0, g], qg)
        m = jnp.max(st, axis=0, keepdims=True)
        p = jnp.exp2(st - m)
        l = jnp.sum(p, axis=0, keepdims=True)
        ot = _dot(vt_ref[0, g * HEAD_DIM:(g + 1) * HEAD_DIM, :], p.astype(BF16)) / l
        for hh, h in enumerate(heads):
            ot_ref[0, h * HEAD_DIM:(h + 1) * HEAD_DIM, :] = ot[:, hh * tq:(hh + 1) * tq].astype(ot_ref.dtype)


def _attention(qt, k, vt, tq_pref=256):
    b, _, t = qt.shape
    tk = k.shape[2]
    tq = _row_tile(t, tq_pref)
    return pl.pallas_call(
        _attn_kernel,
        grid=(b, t // tq),
        in_specs=[
            pl.BlockSpec((1, ATTN_WIDTH, tq), lambda bi, i: (bi, 0, i)),
            pl.BlockSpec((1, ATTN_KV_HEADS, tk, HEAD_DIM), lambda bi, i: (bi, 0, 0, 0)),
            pl.BlockSpec((1, KV_WIDTH, tk), lambda bi, i: (bi, 0, 0)),
        ],
        out_specs=pl.BlockSpec((1, ATTN_WIDTH, tq), lambda bi, i: (bi, 0, i)),
        out_shape=jax.ShapeDtypeStruct((b, ATTN_WIDTH, t), BF16),
        compiler_params=_cparams("parallel", "parallel"),
        name="attention",
    )(qt, k, vt)


def _conformer_kernel(a_ref, g_ref, ap_ref, gp_ref, an_ref, gn_ref, dw_ref, db_ref, lg_ref, lb_ref,
                      o_ref, u_ref, sh_ref, *, tm):
    i = pl.program_id(1)
    nt = pl.num_programs(1)
    u_ref[0:HALO] = jnp.where(i == 0, 0.0, ap_ref[0] * jax.nn.sigmoid(gp_ref[0]))
    u_ref[HALO:HALO + tm] = a_ref[0] * jax.nn.sigmoid(g_ref[0])
    u_ref[HALO + tm:2 * HALO + tm] = jnp.where(i == nt - 1, 0.0, an_ref[0] * jax.nn.sigmoid(gn_ref[0]))
    span = sh_ref.shape[1]
    for p in range(V7X_SUBLANES):
        sh_ref[p] = u_ref[pl.ds(p, span), :]
    half = CONV_KERNEL // 2
    acc = jnp.zeros((tm, u_ref.shape[1]), F32) + db_ref[...]
    for j in range(CONV_KERNEL):
        start = HALO - half + j
        base = start - start % V7X_SUBLANES
        acc = acc + sh_ref[start % V7X_SUBLANES, base:base + tm, :] * dw_ref[j:j + 1, :]
    mean = jnp.mean(acc, axis=-1, keepdims=True)
    cen = acc - mean
    var = jnp.mean(cen * cen, axis=-1, keepdims=True)
    y = cen * lax.rsqrt(var + LN_EPS) * lg_ref[...] + lb_ref[...]
    o_ref[0] = _silu(y).astype(o_ref.dtype)


def _conformer_conv(proj, dw, dw_b, ln_g, ln_b, tm_pref=256):
    b, t, _ = proj.shape
    cw = dw.shape[1]
    tm = _row_tile(t, tm_pref)
    hb = tm // HALO
    nhb = t // HALO
    dwp = jnp.zeros((32, cw), F32).at[:CONV_KERNEL].set(dw)
    prev = lambda c: (lambda bi, i: (bi, jnp.maximum(i * hb - 1, 0), c))
    nxt = lambda c: (lambda bi, i: (bi, jnp.minimum((i + 1) * hb, nhb - 1), c))
    vec = pl.BlockSpec((1, cw), lambda bi, i: (0, 0))
    return pl.pallas_call(
        functools.partial(_conformer_kernel, tm=tm),
        grid=(b, t // tm),
        in_specs=[
            pl.BlockSpec((1, tm, cw), lambda bi, i: (bi, i, 1)),
            pl.BlockSpec((1, tm, cw), lambda bi, i: (bi, i, 2)),
            pl.BlockSpec((1, HALO, cw), prev(1)),
            pl.BlockSpec((1, HALO, cw), prev(2)),
            pl.BlockSpec((1, HALO, cw), nxt(1)),
            pl.BlockSpec((1, HALO, cw), nxt(2)),
            pl.BlockSpec((32, cw), lambda bi, i: (0, 0)),
            vec, vec, vec,
        ],
        out_specs=pl.BlockSpec((1, tm, cw), lambda bi, i: (bi, i, 0)),
        out_shape=jax.ShapeDtypeStruct((b, t, cw), BF16),
        scratch_shapes=[pltpu.VMEM((tm + 2 * HALO, cw), F32),
                        pltpu.VMEM((V7X_SUBLANES, tm + 2 * HALO - V7X_SUBLANES, cw), F32)],
        compiler_params=_cparams("parallel", "parallel"),
        name="conformer_conv",
    )(proj, proj, proj, proj, proj, proj, dwp, dw_b.reshape(1, cw), ln_g.reshape(1, cw), ln_b.reshape(1, cw))


def _hyb_out_kernel(x_ref, at_ref, c_ref, gt_ref, wa_ref, wc_ref, o_ref):
    y = _dot_tn(at_ref[0], wa_ref[...]) + _dot(c_ref[0], wc_ref[...])
    o_ref[0] = x_ref[0] + gt_ref[0] * y


def _hybrid_out(x, attn_t, conv, gate, w_out, tm_pref=512):
    b, t, d = x.shape
    tm = _row_tile(t, tm_pref)
    aw = attn_t.shape[1]
    cw = conv.shape[-1]
    return pl.pallas_call(
        _hyb_out_kernel,
        grid=(b, t // tm),
        in_specs=[
            pl.BlockSpec((1, tm, d), lambda bi, i: (bi, i, 0)),
            pl.BlockSpec((1, aw, tm), lambda bi, i: (bi, 0, i)),
            pl.BlockSpec((1, tm, cw), lambda bi, i: (bi, i, 0)),
            pl.BlockSpec((1, 1, d), lambda bi, i: (bi, 0, 0)),
            pl.BlockSpec((aw, d), lambda bi, i: (0, 0)),
            pl.BlockSpec((cw, d), lambda bi, i: (0, 0)),
        ],
        out_specs=pl.BlockSpec((1, tm, d), lambda bi, i: (bi, i, 0)),
        out_shape=jax.ShapeDtypeStruct((b, t, d), F32),
        compiler_params=_cparams("parallel", "parallel"),
        name="hybrid_out",
    )(x, attn_t, conv, gate, w_out[:aw], w_out[aw:])


def _rwkv_feat_kernel(x_ref, xp_ref, xn_ref, g_ref, sh_ref, sc_ref, mu_ref, wr_ref, wk_ref, wv_ref,
                      g1_ref, g2_ref, w1_ref, w2f_ref, w2r_ref, a1_ref, a2f_ref, a2r_ref,
                      w0_ref, a0_ref, kk_ref, ka_ref, u_ref, ones_ref,
                      r_o, v_o, kkn_o, gg_o, bonus_o, lwf_o, kdf_o, bf_o, lwr_o, kdr_o, br_o, *, tm):
    i = pl.program_id(1)
    nt = pl.num_programs(1)
    g, sh, sc = g_ref[...], sh_ref[0], sc_ref[0]
    h = _norm_mod(x_ref[0], g, sh, sc)
    hp = jnp.where(i == 0, 0.0, _norm_mod(xp_ref[0, HALO - 1:HALO, :], g, sh, sc))
    hn = jnp.where(i == nt - 1, 0.0, _norm_mod(xn_ref[0, 0:1, :], g, sh, sc))
    rows = lax.broadcasted_iota(jnp.int32, (tm, 1), 0)
    up = jnp.where(rows == 0, hp, pltpu.roll(h, 1, 0))
    dn = jnp.where(rows == tm - 1, hn, pltpu.roll(h, tm - 1, 0))
    xx = 0.5 * (up + dn) - h

    def mix(n):
        return (h + xx * mu_ref[n:n + 1, :]).astype(BF16)

    r = _dot(mix(0), wr_ref[...])
    k = _dot(mix(2), wk_ref[...])
    v = _dot(mix(3), wv_ref[...])
    gg = _dot(jax.nn.sigmoid(_dot(mix(5), g1_ref[...])).astype(BF16), g2_ref[...])
    tl = jnp.tanh(_dot(mix(1), w1_ref[...])).astype(BF16)
    al = _dot(mix(4), a1_ref[...]).astype(BF16)

    ones_bd = ones_ref[...]
    kkf = k * kk_ref[...]
    kkn = kkf / jnp.maximum(jnp.sqrt(_head_sum(kkf * kkf, ones_bd)), 1e-12)
    r_o[0] = r
    v_o[0] = v
    kkn_o[0] = kkn
    gg_o[0] = gg

    bonus = jnp.zeros_like(r)
    outs = ((w2f_ref, a2f_ref, lwf_o, kdf_o, bf_o), (w2r_ref, a2r_ref, lwr_o, kdr_o, br_o))
    for dd, (w2_ref, a2_ref, lw_o, kd_o, b_o) in enumerate(outs):
        z = -(w0_ref[dd:dd + 1, :] + _dot(tl, w2_ref[...]))
        softplus = jnp.maximum(z, 0.0) + jnp.log(1.0 + jnp.exp(-jnp.abs(z)))
        lw_o[0] = -jnp.exp(-softplus - 0.5)
        a = jax.nn.sigmoid(a0_ref[dd:dd + 1, :] + _dot(al, a2_ref[...]))
        kd = k * (1.0 + (a - 1.0) * ka_ref[...])
        kd_o[0] = kd
        b_o[0] = kkn * a
        bonus = bonus + _head_sum(r * kd * u_ref[dd:dd + 1, :], ones_bd) * v
    bonus_o[0] = bonus


def _rwkv_features(x, g, shift, scale, p, ones_bd, tm_pref=256):
    b, t, d = x.shape
    tm = _row_tile(t, tm_pref)
    hb = tm // HALO
    nhb = t // HALO
    row = pl.BlockSpec((1, tm, d), lambda bi, i: (bi, i, 0))
    vec3 = pl.BlockSpec((1, 1, d), lambda bi, i: (bi, 0, 0))

    def full(a):
        return pl.BlockSpec(a.shape, lambda bi, i: (0,) * a.ndim)

    consts = [p["mu"], p["wr"], p["wk"], p["wv"], p["g1"], p["g2"], p["w1"], p["w2f"], p["w2r"],
              p["a1"], p["a2f"], p["a2r"], p["w0"], p["a0"], p["kk"], p["ka"], p["u"], ones_bd]
    n_out = 11
    return pl.pallas_call(
        functools.partial(_rwkv_feat_kernel, tm=tm),
        grid=(b, t // tm),
        in_specs=[
            row,
            pl.BlockSpec((1, HALO, d), lambda bi, i: (bi, jnp.maximum(i * hb - 1, 0), 0)),
            pl.BlockSpec((1, HALO, d), lambda bi, i: (bi, jnp.minimum((i + 1) * hb, nhb - 1), 0)),
            pl.BlockSpec((1, d), lambda bi, i: (0, 0)),
            vec3, vec3,
        ] + [full(a) for a in consts],
        out_specs=[row] * n_out,
        out_shape=[jax.ShapeDtypeStruct((b, t, d), F32)] * n_out,
        compiler_params=_cparams("parallel", "parallel"),
        name="rwkv_features",
    )(x, x, x, g.reshape(1, d), shift, scale, *consts)


def _split_bf16(x, n):
    parts, rem = [], x
    for _ in range(n):
        p = rem.astype(BF16)
        parts.append(p)
        rem = rem - p.astype(F32)
    return parts


def _mm_parts(a_parts, b_parts, fn):
    n = max(len(a_parts), len(b_parts))
    acc = None
    for i in reversed(range(len(a_parts))):
        for j in reversed(range(len(b_parts))):
            if i + j < n:
                term = fn(a_parts[i], b_parts[j])
                acc = term if acc is None else acc + term
    return acc


def _bdot(a, b):
    return lax.dot_general(a, b, (((2,), (1,)), ((0,), (0,))), preferred_element_type=F32)


def _bdot_nt(a, b):
    return lax.dot_general(a, b, (((2,), (2,)), ((0,), (0,))), preferred_element_type=F32)


def _bdot_tn(a, b):
    return lax.dot_general(a, b, (((1,), (1,)), ((0,), (0,))), preferred_element_type=F32)


def _mm_stacked(a_parts, b_parts, fn, rows):
    if len(a_parts) != 2 or len(b_parts) != 2:
        return _mm_parts(a_parts, b_parts, fn)
    both = fn(jnp.concatenate(a_parts, axis=1), b_parts[0])
    return fn(a_parts[0], b_parts[1]) + both[:, rows:] + both[:, :rows]


def _scan_kernel(rf_ref, vf_ref, kkf_ref, rr_ref, vr_ref, kkr_ref, lwf_ref, kdf_ref, bf_ref,
                 lwr_ref, kdr_ref, br_ref, s0_ref, ms_ref, mi_ref, lvl_ref, bd_ref, eye_ref,
                 yf_ref, yr_ref, sf_ref, st_ref, *, passes):
    c = pl.program_id(1)
    nc = pl.num_programs(1)
    L = rf_ref.shape[1]
    W = V7X_MXU_DIM
    G = rf_ref.shape[2] // W

    @pl.when(c == 0)
    def _():
        st_ref[...] = s0_ref[0]

    mask_strict = ms_ref[...]
    mask_incl = mi_ref[...]
    bd = bd_ref[...][None]
    bd16 = bd.astype(BF16)
    eye = eye_ref[...]
    eye_row = jnp.concatenate([eye[:L, :L]] * SCAN_GROUP, axis=1)[None]

    def groups(a):
        return jnp.stack([a[:, g * W:(g + 1) * W] for g in range(G)], axis=0)

    def expand(parts):
        return [jnp.concatenate([p] * SCAN_GROUP, axis=1) * bd16 for p in parts]

    def cat(parts_a, parts_b):
        return [jnp.concatenate([a, b], axis=1) for a, b in zip(parts_a, parts_b)]

    def prep(r_ref, v_ref, kk_ref, lw_ref, kd_ref, b_ref, tri, last):
        lw = lw_ref[0]
        cs = _mm_parts([tri.astype(BF16)], _split_bf16(lw, passes["cs"]), _dot)
        gam = jnp.exp(cs)
        gam_inv = jnp.exp(-cs)
        gam_end = gam[last:last + 1, :]
        kt = kd_ref[0] * gam_inv
        bt = b_ref[0] * gam_inv
        return dict(kkg=groups(kk_ref[0] * jnp.exp(cs - lw)), rg=groups(r_ref[0] * gam), kt=groups(kt),
                    bt=groups(bt), kb=groups(jnp.concatenate([kt * gam_end, -(bt * gam_end)], axis=0)),
                    v=groups(v_ref[0]), gend=groups(gam_end))

    fw = prep(rf_ref, vf_ref, kkf_ref, lwf_ref, kdf_ref, bf_ref, mi_ref[0, :, :L], L - 1)
    rv = prep(rr_ref, vr_ref, kkr_ref, lwr_ref, kdr_ref, br_ref, mi_ref[G, :, :L], 0)
    both_dirs = {k: jnp.concatenate([fw[k], rv[k]], axis=0) for k in fw}
    kkg, rg, kt, bt, kb, v, gend = (both_dirs[k] for k in ("kkg", "rg", "kt", "bt", "kb", "v", "gend"))
    st = st_ref[...]

    pa, pb = passes["sc"]
    lhs = jnp.concatenate([kkg, rg], axis=1)
    lhs_p = _split_bf16(lhs, max(pa, passes["sprod"][0]))
    rhs_p = cat(expand(_split_bf16(bt, pb)), expand(_split_bf16(kt, pb)))
    sc = _mm_parts(lhs_p[:pa], rhs_p, _bdot_nt)
    m_row = sc[:, :L, :W] * mask_strict
    n_row = sc[:, :L, W:] * mask_strict
    ab_row = sc[:, L:, :W] * mask_incl
    ak_row = sc[:, L:, W:] * mask_incl

    pa, pb = passes["inv"]
    t_row = eye_row - m_row * lvl_ref[0][None]
    for lv in range(1, lvl_ref.shape[0]):
        off_p = expand(_split_bf16(m_row * lvl_ref[lv][None], pb))
        x = _mm_stacked(_split_bf16(t_row, pa), off_p, _bdot, L)
        t_row = t_row - _mm_stacked(_split_bf16(x, pa), expand(_split_bf16(t_row, pb)), _bdot, L)

    pa, pb = passes["sprod"]
    s_prod = _mm_parts(lhs_p[:pa], _split_bf16(st, pb), _bdot)
    pa, pb = passes["nv"]
    v_p = _split_bf16(v, max(pb, passes["upd"][1]))
    nv = _mm_parts(_split_bf16(jnp.concatenate([n_row, ak_row], axis=1), pa), expand(v_p[:pb]), _bdot)
    pa, pb = passes["u"]
    u = _mm_parts(_split_bf16(t_row, pa), expand(_split_bf16(s_prod[:, :L] + nv[:, :L], pb)), _bdot)
    pa, pb = passes["abu"]
    u_p = _split_bf16(u, max(pb, passes["upd"][1]))
    y = s_prod[:, L:] + nv[:, L:] - _mm_parts(_split_bf16(ab_row, pa), expand(u_p[:pb]), _bdot)
    for g in range(G):
        yf_ref[0, :, g * W:(g + 1) * W] = y[g]
        yr_ref[0, :, g * W:(g + 1) * W] = y[G + g]

    pa, pb = passes["upd"]
    upd = _mm_parts(_split_bf16(kb, pa), cat(v_p[:pb], u_p[:pb]), _bdot_tn) * bd
    gcol = jnp.sum(eye[None] * gend, axis=2, keepdims=True)
    st_ref[...] = st * gcol + upd

    @pl.when(c == nc - 1)
    def _():
        sf_ref[0] = st_ref[...]


def _scan_masks(L, groups):
    t = jnp.arange(L)[:, None]
    i = jnp.arange(L)[None, :]
    tile = lambda m: jnp.broadcast_to(jnp.tile(m.astype(F32), (1, SCAN_GROUP))[None], (groups, L, SCAN_GROUP * L))
    strict = jnp.concatenate([tile(i < t), tile(i > t)], axis=0)
    incl = jnp.concatenate([tile(i <= t), tile(i >= t)], axis=0)
    sizes = [2 ** k for k in range(L.bit_length() - 1)]
    levels = jnp.stack([jnp.tile(((t // (2 * s) == i // (2 * s)) & (t // s != i // s)).astype(F32),
                                 (1, SCAN_GROUP)) for s in sizes], axis=0)
    return strict, incl, levels


def _wkv_scan(r, v, kk, lw_f, kd_f, b_f, lw_r, kd_r, b_r, s0):
    b, t, d = r.shape
    L = SCAN_CHUNK
    nc = t // L
    W = V7X_MXU_DIM
    ng = d // W
    ms, mi, lvl = _scan_masks(L, ng)
    hid = jnp.arange(W) // HEAD_DIM
    bd = (hid[:, None] == hid[None, :]).astype(F32)
    eye = jnp.eye(W, dtype=F32)
    fwd = pl.BlockSpec((1, L, d), lambda bi, c: (bi, c, 0))
    rev = pl.BlockSpec((1, L, d), lambda bi, c: (bi, nc - 1 - c, 0))
    state = pl.BlockSpec((1, 2 * ng, W, W), lambda bi, c: (bi, 0, 0, 0))
    const = lambda a: pl.BlockSpec(a.shape, lambda bi, c: (0,) * a.ndim)
    return pl.pallas_call(
        functools.partial(_scan_kernel, passes=SCAN_PASSES),
        grid=(b, nc),
        in_specs=[fwd] * 3 + [rev] * 3 + [fwd] * 3 + [rev] * 3 + [state, const(ms), const(mi), const(lvl), const(bd), const(eye)],
        out_specs=[fwd, rev, state],
        out_shape=[jax.ShapeDtypeStruct((b, t, d), F32), jax.ShapeDtypeStruct((b, t, d), F32),
                   jax.ShapeDtypeStruct((b, 2 * ng, W, W), F32)],
        scratch_shapes=[pltpu.VMEM((2 * ng, W, W), F32)],
        compiler_params=_cparams("parallel", "arbitrary"),
        name="wkv_scan",
    )(r, v, kk, r, v, kk, lw_f, kd_f, b_f, lw_r, kd_r, b_r, s0, ms, mi, lvl, bd, eye)


def _rwkv_out_kernel(x_ref, yf_ref, yr_ref, bonus_ref, gg_ref, gt_ref, lg_ref, lb_ref, wo_ref, ones_ref, o_ref):
    ones_bd = ones_ref[...]
    inv = 1.0 / HEAD_DIM
    y = yf_ref[0] + yr_ref[0]
    cen = y - _head_sum(y, ones_bd) * inv
    var = _head_sum(cen * cen, ones_bd) * inv
    yn = cen * lax.rsqrt(var + LNX_EPS) * lg_ref[...] + lb_ref[...]
    z = ((yn + bonus_ref[0]) * gg_ref[0]).astype(BF16)
    o_ref[0] = x_ref[0] + gt_ref[0] * _dot(z, wo_ref[...])


def _rwkv_out(x, yf, yr, bonus, gg, gate, lnx_g, lnx_b, wo, ones_bd, tm_pref=256):
    b, t, d = x.shape
    tm = _row_tile(t, tm_pref)
    row = pl.BlockSpec((1, tm, d), lambda bi, i: (bi, i, 0))
    vec = pl.BlockSpec((1, d), lambda bi, i: (0, 0))
    return pl.pallas_call(
        _rwkv_out_kernel,
        grid=(b, t // tm),
        in_specs=[row] * 5 + [
            pl.BlockSpec((1, 1, d), lambda bi, i: (bi, 0, 0)), vec, vec,
            pl.BlockSpec((d, d), lambda bi, i: (0, 0)),
            pl.BlockSpec((V7X_MXU_DIM, V7X_MXU_DIM), lambda bi, i: (0, 0)),
        ],
        out_specs=row,
        out_shape=jax.ShapeDtypeStruct((b, t, d), F32),
        compiler_params=_cparams("parallel", "parallel"),
        name="rwkv_out",
    )(x, yf, yr, bonus, gg, gate, lnx_g.reshape(1, d), lnx_b.reshape(1, d), wo, ones_bd)


def _rope_tables(t):
    n_freq = HEAD_DIM // 4
    inv = ROPE_THETA ** (-jnp.arange(n_freq, dtype=F32) / n_freq)
    pos = jnp.arange(t, dtype=jnp.int32)
    row = (pos // GRID_W).astype(F32)[:, None] * inv
    col = (pos % GRID_W).astype(F32)[:, None] * inv
    cos = jnp.concatenate([jnp.cos(row)] * 2 + [jnp.cos(col)] * 2, axis=1)
    sin = jnp.concatenate([-jnp.sin(row), jnp.sin(row), -jnp.sin(col), jnp.sin(col)], axis=1)
    return jnp.tile(cos, (1, ATTN_Q_HEADS)), jnp.tile(sin, (1, ATTN_Q_HEADS))


def _hybrid_layer(xc, xl, mc, ml, norm_g, w_in, q_g, k_g, dw, dw_b, ln_g, ln_b, w_out, tables, ones_bd):
    d = xl.shape[-1]
    o1, o2, o3 = ATTN_WIDTH, ATTN_WIDTH + KV_WIDTH, ATTN_WIDTH + 2 * KV_WIDTH
    w_perm = jnp.concatenate([w_in[:, :o1], w_in[:, o3:], w_in[:, o1:o3]], axis=1).astype(BF16)
    w_out16 = w_out.astype(BF16)
    (cos_l, sin_l), (cos_c, sin_c) = tables
    pl_ = _norm_mod_matmul(xl, norm_g, ml[0], ml[1], w_perm)
    pc_ = _norm_mod_matmul(xc, norm_g, mc[0], mc[1], w_perm)
    ql, kl, vl = _qk_prep(pl_, cos_l, sin_l, q_g, k_g, ones_bd)
    qc, kc, vc = _qk_prep(pc_, cos_c, sin_c, q_g, k_g, ones_bd)
    k_all = jnp.concatenate([kc, kl], axis=2)
    v_all = jnp.concatenate([vc, vl], axis=2)
    attn_l = _attention(ql, k_all, v_all)
    attn_c = _attention(qc, kc, vc)
    conv_l = _conformer_conv(pl_, dw, dw_b, ln_g, ln_b)
    conv_c = _conformer_conv(pc_, dw, dw_b, ln_g, ln_b)
    xl = _hybrid_out(xl, attn_l, conv_l, ml[2], w_out16)
    xc = _hybrid_out(xc, attn_c, conv_c, mc[2], w_out16)
    return xc, xl


def _rwkv_layer(xc, xl, mc, ml, norm_g, mu, wr, wk, wv, wo, w0, w1, w2, a0, a1, a2, g1, g2,
                k_k, k_a, u, lnx_g, lnx_b, ones_bd, ctx_out):
    b, _, d = xl.shape
    lora_w = w1.shape[-1]
    zeros_w = jnp.zeros((lora_w, d), F32)
    p = {
        "mu": jnp.zeros((V7X_SUBLANES, d), F32).at[:6].set(mu),
        "wr": wr.astype(BF16), "wk": wk.astype(BF16), "wv": wv.astype(BF16),
        "g1": g1.astype(BF16), "g2": g2.astype(BF16),
        "w1": jnp.concatenate([w1[0], w1[1]], axis=1).astype(BF16),
        "w2f": jnp.concatenate([w2[0], zeros_w], axis=0).astype(BF16),
        "w2r": jnp.concatenate([zeros_w, w2[1]], axis=0).astype(BF16),
        "a1": jnp.concatenate([a1[0], a1[1]], axis=1).astype(BF16),
        "a2f": jnp.concatenate([a2[0], jnp.zeros_like(a2[1])], axis=0).astype(BF16),
        "a2r": jnp.concatenate([jnp.zeros_like(a2[0]), a2[1]], axis=0).astype(BF16),
        "w0": jnp.zeros((V7X_SUBLANES, d), F32).at[:2].set(w0),
        "a0": jnp.zeros((V7X_SUBLANES, d), F32).at[:2].set(a0),
        "kk": k_k.reshape(1, d), "ka": k_a.reshape(1, d),
        "u": jnp.zeros((V7X_SUBLANES, d), F32).at[:2].set(u.reshape(2, d)),
    }
    fc = _rwkv_features(xc, norm_g, mc[0], mc[1], p, ones_bd)
    fl = _rwkv_features(xl, norm_g, ml[0], ml[1], p, ones_bd)
    ng = d // V7X_MXU_DIM
    state = jnp.zeros((b, 2 * ng, V7X_MXU_DIM, V7X_MXU_DIM), F32)
    ys = {}
    for name, f in (("c", fc), ("l", fl)):
        r_, v_, kkn = f[:3]
        y_f, y_r, state = _wkv_scan(r_, v_, kkn, f[5], f[6], f[7], f[8], f[9], f[10], state)
        ys[(name, 0)], ys[(name, 1)] = y_f, y_r
    wo16 = wo.astype(BF16)
    xl = _rwkv_out(xl, ys[("l", 0)], ys[("l", 1)], fl[4], fl[3], ml[2], lnx_g, lnx_b, wo16, ones_bd)
    if ctx_out:
        xc = _rwkv_out(xc, ys[("c", 0)], ys[("c", 1)], fc[4], fc[3], mc[2], lnx_g, lnx_b, wo16, ones_bd)
    return xc, xl


def kernel(x, c, ctx, c_ctx, mod_w, mod_b, norm_mix, norm_ffn, ffn_w_in, ffn_dw, ffn_dw_b, ffn_w_out, hyb_w_in, hyb_q_norm, hyb_k_norm, hyb_dw, hyb_dw_b, hyb_ln_g, hyb_ln_b, hyb_w_out, rwkv_mu, rwkv_wr, rwkv_wk, rwkv_wv, rwkv_wo, rwkv_w0, rwkv_w1, rwkv_w2, rwkv_a0, rwkv_a1, rwkv_a2, rwkv_g1, rwkv_g2, rwkv_kk, rwkv_ka, rwkv_u, rwkv_lnx_g, rwkv_lnx_b, final_norm):
    b, t, d = x.shape
    n_ctx = ctx.shape[1]
    depth = mod_w.shape[0]
    assert d % V7X_MXU_DIM == 0 and t % SCAN_CHUNK == 0 and n_ctx % SCAN_CHUNK == 0
    assert t % GRID_W == 0 and t % HALO == 0 and n_ctx % HALO == 0

    m_rows = -(-(b + 1) // V7X_SUBLANES) * V7X_SUBLANES
    cvec = jnp.zeros((m_rows, d), F32).at[:b].set(c).at[b].set(c_ctx)
    mod = _modulation(cvec, mod_w, mod_b)
    mod = mod.reshape(depth, m_rows, 6, d)

    hid = jnp.arange(V7X_MXU_DIM) // HEAD_DIM
    ones_bd = (hid[:, None] == hid[None, :]).astype(BF16)
    cos_l, sin_l = _rope_tables(t)
    tables = ((cos_l, sin_l), (jnp.ones((n_ctx, ATTN_WIDTH), F32), jnp.zeros((n_ctx, ATTN_WIDTH), F32)))

    xl, xc = x, ctx
    for i in range(depth):
        last = i == depth - 1
        j = i // 2
        ml = [mod[i, :b, n][:, None, :] for n in range(6)]
        mc = [jnp.broadcast_to(mod[i, b, n][None, None, :], (b, 1, d)) for n in range(6)]
        if i % 2 == 0:
            xc, xl = _hybrid_layer(xc, xl, mc, ml, norm_mix[i], hyb_w_in[j], hyb_q_norm[j], hyb_k_norm[j],
                                   hyb_dw[j], hyb_dw_b[j], hyb_ln_g[j], hyb_ln_b[j], hyb_w_out[j],
                                   tables, ones_bd)
        else:
            xc, xl = _rwkv_layer(xc, xl, mc, ml, norm_mix[i], rwkv_mu[j], rwkv_wr[j], rwkv_wk[j], rwkv_wv[j],
                                 rwkv_wo[j], rwkv_w0[j], rwkv_w1[j], rwkv_w2[j], rwkv_a0[j], rwkv_a1[j],
                                 rwkv_a2[j], rwkv_g1[j], rwkv_g2[j], rwkv_kk[j], rwkv_ka[j], rwkv_u[j],
                                 rwkv_lnx_g[j], rwkv_lnx_b[j], ones_bd, not last)
        w_in16 = ffn_w_in[i].astype(BF16)
        w_out16 = ffn_w_out[i].astype(BF16)
        xl = _conv_ffn(xl, norm_ffn[i], ml[3], ml[4], ml[5], w_in16, ffn_dw[i], ffn_dw_b[i], w_out16,
                       final_g=final_norm if last else None)
        if not last:
            xc = _conv_ffn(xc, norm_ffn[i], mc[3], mc[4], mc[5], w_in16, ffn_dw[i], ffn_dw_b[i], w_out16)
    return xl
```

```python
import functools

import jax
import jax.numpy as jnp
from jax import lax
from jax.experimental import pallas as pl
from jax.experimental.pallas import tpu as pltpu

F32 = jnp.float32
BF16 = jnp.bfloat16
HIGHEST = lax.Precision.HIGHEST

HEAD_DIM = 64
ATTN_Q_HEADS = 8
ATTN_KV_HEADS = 2
ATTN_GROUP = ATTN_Q_HEADS // ATTN_KV_HEADS
ATTN_WIDTH = ATTN_Q_HEADS * HEAD_DIM
KV_WIDTH = ATTN_KV_HEADS * HEAD_DIM
CONV_KERNEL = 31
GRID_W = 64
ROPE_THETA = 10000.0
NORM_EPS = 1e-6
LN_EPS = 1e-5
LNX_EPS = 64e-5
LOG2_E = 1.4426950408889634

V7X_LANES = 128
V7X_SUBLANES = 8
V7X_MXU_DIM = 256
V7X_VMEM_BYTES = 64 * 1024 * 1024
VMEM_LIMIT_BYTES = 58 * 1024 * 1024

HALO = 16
SCAN_CHUNK = 64
SCAN_GROUP = V7X_MXU_DIM // HEAD_DIM
SCAN_PASSES = {"cs": 2, "sc": (1, 1), "inv": (1, 1), "sprod": (1, 1), "nv": (1, 1), "u": (1, 1),
               "abu": (1, 1), "upd": (1, 1)}


def _cparams(*sem):
    return pltpu.CompilerParams(dimension_semantics=sem, vmem_limit_bytes=VMEM_LIMIT_BYTES)


def _row_tile(t, pref):
    return pref if t % pref == 0 else t


def _dot(a, b, precision=None):
    return jnp.dot(a, b, preferred_element_type=F32, precision=precision)


def _dot_nt(a, b, precision=None):
    return lax.dot_general(a, b, (((1,), (1,)), ((), ())), preferred_element_type=F32, precision=precision)


def _dot_tn(a, b, precision=None):
    return lax.dot_general(a, b, (((0,), (0,)), ((), ())), preferred_element_type=F32, precision=precision)


def _silu(x):
    return x * jax.nn.sigmoid(x)


def _norm_mod(x, g, shift, scale):
    ms = jnp.mean(x * x, axis=-1, keepdims=True)
    return (x * lax.rsqrt(ms + NORM_EPS) * g) * (1.0 + scale) + shift


def _head_sum(x, ones_bd):
    w = x.shape[-1]
    hi = x.astype(BF16)
    lo = (x - hi.astype(F32)).astype(BF16)
    outs = []
    for s in range(0, w, V7X_MXU_DIM):
        e = min(s + V7X_MXU_DIM, w)
        g = ones_bd[: e - s, : e - s]
        outs.append(_dot(hi[:, s:e], g) + _dot(lo[:, s:e], g))
    return outs[0] if len(outs) == 1 else jnp.concatenate(outs, axis=-1)


def _mod_kernel(c_ref, w_ref, b_ref, o_ref):
    s = _silu(c_ref[...])
    o_ref[0] = _dot(s, w_ref[0], HIGHEST) + b_ref[0]


def _modulation(cvec, mod_w, mod_b):
    depth, d, n = mod_w.shape
    m = cvec.shape[0]
    tn = 512
    return pl.pallas_call(
        _mod_kernel,
        grid=(depth, n // tn),
        in_specs=[
            pl.BlockSpec((m, d), lambda l, j: (0, 0)),
            pl.BlockSpec((1, d, tn), lambda l, j: (l, 0, j)),
            pl.BlockSpec((1, 1, tn), lambda l, j: (l, 0, j)),
        ],
        out_specs=pl.BlockSpec((1, m, tn), lambda l, j: (l, 0, j)),
        out_shape=jax.ShapeDtypeStruct((depth, m, n), F32),
        compiler_params=_cparams("parallel", "parallel"),
        name="modulation",
    )(cvec, mod_w, mod_b.reshape(depth, 1, n))


def _nmm_kernel(x_ref, g_ref, sh_ref, sc_ref, w_ref, o_ref):
    h = _norm_mod(x_ref[0], g_ref[...], sh_ref[0], sc_ref[0]).astype(BF16)
    o_ref[0] = _dot(h, w_ref[...]).astype(o_ref.dtype)


def _norm_mod_matmul(x, g, shift, scale, w, tm_pref=512, out_dtype=F32):
    b, t, d = x.shape
    n = w.shape[1]
    tm = _row_tile(t, tm_pref)
    return pl.pallas_call(
        _nmm_kernel,
        grid=(b, t // tm),
        in_specs=[
            pl.BlockSpec((1, tm, d), lambda bi, i: (bi, i, 0)),
            pl.BlockSpec((1, d), lambda bi, i: (0, 0)),
            pl.BlockSpec((1, 1, d), lambda bi, i: (bi, 0, 0)),
            pl.BlockSpec((1, 1, d), lambda bi, i: (bi, 0, 0)),
            pl.BlockSpec((d, n), lambda bi, i: (0, 0)),
        ],
        out_specs=pl.BlockSpec((1, tm, n), lambda bi, i: (bi, i, 0)),
        out_shape=jax.ShapeDtypeStruct((b, t, n), out_dtype),
        compiler_params=_cparams("parallel", "parallel"),
        name="norm_mod_matmul",
    )(x, g.reshape(1, d), shift, scale, w)


def _ffn_kernel(x_ref, xp_ref, xn_ref, g_ref, sh_ref, sc_ref, gt_ref, wg_ref, wv_ref,
                dw_ref, db_ref, wo_ref, fg_ref, o_ref, h_ref, gate_ref, acc_ref, *, tm, final_norm):
    i = pl.program_id(1)
    j = pl.program_id(2)
    nt = pl.num_programs(1)
    nf = pl.num_programs(2)

    @pl.when(j == 0)
    def _():
        g, sh, sc = g_ref[...], sh_ref[0], sc_ref[0]
        h_ref[0:HALO] = _norm_mod(xp_ref[0], g, sh, sc).astype(BF16)
        h_ref[HALO:HALO + tm] = _norm_mod(x_ref[0], g, sh, sc).astype(BF16)
        h_ref[HALO + tm:2 * HALO + tm] = _norm_mod(xn_ref[0], g, sh, sc).astype(BF16)
        acc_ref[...] = jnp.zeros_like(acc_ref)

    gate_ref[...] = _dot(h_ref[...], wg_ref[...])
    val = _dot(h_ref[HALO:HALO + tm], wv_ref[...])
    rows = lax.broadcasted_iota(jnp.int32, (tm, 1), 0)
    g_prev = jnp.where((rows == 0) & (i == 0), 0.0, gate_ref[pl.ds(HALO - 1, tm), :])
    g_next = jnp.where((rows == tm - 1) & (i == nt - 1), 0.0, gate_ref[pl.ds(HALO + 1, tm), :])
    conv = (g_prev * dw_ref[0:1, :] + gate_ref[pl.ds(HALO, tm), :] * dw_ref[1:2, :]
            + g_next * dw_ref[2:3, :] + db_ref[...])
    act = (_silu(conv) * val).astype(BF16)
    acc_ref[...] += _dot(act, wo_ref[...])

    @pl.when(j == nf - 1)
    def _():
        y = x_ref[0] + gt_ref[0] * acc_ref[...]
        if final_norm:
            ms = jnp.mean(y * y, axis=-1, keepdims=True)
            y = y * lax.rsqrt(ms + NORM_EPS) * fg_ref[...]
        o_ref[0] = y


def _conv_ffn(x, g, shift, scale, gate, w_in, dw, dw_b, w_out, final_g=None, tm_pref=512):
    b, t, d = x.shape
    f = w_out.shape[0]
    nf = 2
    fc = f // nf
    tm = _row_tile(t, tm_pref)
    hb = tm // HALO
    nhb = t // HALO
    dwp = jnp.zeros((V7X_SUBLANES, f), F32).at[:dw.shape[0]].set(dw)
    fg = jnp.ones((d,), F32) if final_g is None else final_g
    return pl.pallas_call(
        functools.partial(_ffn_kernel, tm=tm, final_norm=final_g is not None),
        grid=(b, t // tm, nf),
        in_specs=[
            pl.BlockSpec((1, tm, d), lambda bi, i, j: (bi, i, 0)),
            pl.BlockSpec((1, HALO, d), lambda bi, i, j: (bi, jnp.maximum(i * hb - 1, 0), 0)),
            pl.BlockSpec((1, HALO, d), lambda bi, i, j: (bi, jnp.minimum((i + 1) * hb, nhb - 1), 0)),
            pl.BlockSpec((1, d), lambda bi, i, j: (0, 0)),
            pl.BlockSpec((1, 1, d), lambda bi, i, j: (bi, 0, 0)),
            pl.BlockSpec((1, 1, d), lambda bi, i, j: (bi, 0, 0)),
            pl.BlockSpec((1, 1, d), lambda bi, i, j: (bi, 0, 0)),
            pl.BlockSpec((d, fc), lambda bi, i, j: (0, j)),
            pl.BlockSpec((d, fc), lambda bi, i, j: (0, nf + j)),
            pl.BlockSpec((V7X_SUBLANES, fc), lambda bi, i, j: (0, j)),
            pl.BlockSpec((1, fc), lambda bi, i, j: (0, j)),
            pl.BlockSpec((fc, d), lambda bi, i, j: (j, 0)),
            pl.BlockSpec((1, d), lambda bi, i, j: (0, 0)),
        ],
        out_specs=pl.BlockSpec((1, tm, d), lambda bi, i, j: (bi, i, 0)),
        out_shape=jax.ShapeDtypeStruct((b, t, d), F32),
        scratch_shapes=[
            pltpu.VMEM((tm + 2 * HALO, d), BF16),
            pltpu.VMEM((tm + 2 * HALO, fc), F32),
            pltpu.VMEM((tm, d), F32),
        ],
        compiler_params=_cparams("parallel", "parallel", "arbitrary"),
        name="conv_ffn",
    )(x, x, x, g.reshape(1, d), shift, scale, gate, w_in, w_in, dwp, dw_b.reshape(1, f), w_out, fg.reshape(1, d))


def _rope(x, cos, sin):
    w = x.shape[-1]
    lane = lax.broadcasted_iota(jnp.int32, x.shape, x.ndim - 1)
    partner = jnp.where(lane % 32 < 16, pltpu.roll(x, w - 16, x.ndim - 1), pltpu.roll(x, 16, x.ndim - 1))
    return x * cos + partner * sin


def _qk_prep_kernel(q_ref, k_ref, v_ref, cos_ref, sin_ref, qg_ref, kg_ref, ones_ref,
                    qo_ref, ko_ref, vo_ref):
    ones_bd = ones_ref[...]
    inv = 1.0 / HEAD_DIM
    q = q_ref[0]
    q = q * lax.rsqrt(_head_sum(q * q, ones_bd) * inv + NORM_EPS) * qg_ref[...]
    q = _rope(q, cos_ref[...], sin_ref[...])
    qo_ref[0] = (q * (HEAD_DIM ** -0.5 * LOG2_E)).T.astype(BF16)
    k = k_ref[0]
    k = k * lax.rsqrt(_head_sum(k * k, ones_bd) * inv + NORM_EPS) * kg_ref[...]
    k = _rope(k, cos_ref[:, :KV_WIDTH], sin_ref[:, :KV_WIDTH])
    for g in range(ATTN_KV_HEADS):
        ko_ref[0, g] = k[:, g * HEAD_DIM:(g + 1) * HEAD_DIM].astype(BF16)
    vo_ref[0] = v_ref[0].T.astype(BF16)


def _qk_prep(proj, cos, sin, q_g, k_g, ones_bd, tm_pref=512):
    b, t, _ = proj.shape
    tm = _row_tile(t, tm_pref)
    kcol = 3 * ATTN_WIDTH // KV_WIDTH
    return pl.pallas_call(
        _qk_prep_kernel,
        grid=(b, t // tm),
        in_specs=[
            pl.BlockSpec((1, tm, ATTN_WIDTH), lambda bi, i: (bi, i, 0)),
            pl.BlockSpec((1, tm, KV_WIDTH), lambda bi, i: (bi, i, kcol)),
            pl.BlockSpec((1, tm, KV_WIDTH), lambda bi, i: (bi, i, kcol + 1)),
            pl.BlockSpec((tm, ATTN_WIDTH), lambda bi, i: (i, 0)),
            pl.BlockSpec((tm, ATTN_WIDTH), lambda bi, i: (i, 0)),
            pl.BlockSpec((1, ATTN_WIDTH), lambda bi, i: (0, 0)),
            pl.BlockSpec((1, KV_WIDTH), lambda bi, i: (0, 0)),
            pl.BlockSpec((V7X_MXU_DIM, V7X_MXU_DIM), lambda bi, i: (0, 0)),
        ],
        out_specs=[
            pl.BlockSpec((1, ATTN_WIDTH, tm), lambda bi, i: (bi, 0, i)),
            pl.BlockSpec((1, ATTN_KV_HEADS, tm, HEAD_DIM), lambda bi, i: (bi, 0, i, 0)),
            pl.BlockSpec((1, KV_WIDTH, tm), lambda bi, i: (bi, 0, i)),
        ],
        out_shape=[
            jax.ShapeDtypeStruct((b, ATTN_WIDTH, t), BF16),
            jax.ShapeDtypeStruct((b, ATTN_KV_HEADS, t, HEAD_DIM), BF16),
            jax.ShapeDtypeStruct((b, KV_WIDTH, t), BF16),
        ],
        compiler_params=_cparams("parallel", "parallel"),
        name="qk_prep",
    )(proj, proj, proj, cos, sin, jnp.tile(q_g, ATTN_Q_HEADS).reshape(1, ATTN_WIDTH),
      jnp.tile(k_g, ATTN_KV_HEADS).reshape(1, KV_WIDTH), ones_bd)


def _attn_kernel(qt_ref, k_ref, vt_ref, ot_ref):
    tq = qt_ref.shape[2]
    for g in range(ATTN_KV_HEADS):
        heads = [g * ATTN_GROUP + hh for hh in range(ATTN_GROUP)]
        qg = jnp.concatenate([qt_ref[0, h * HEAD_DIM:(h + 1) * HEAD_DIM, :] for h in heads], axis=1)
        st = _dot(k_ref[0, g], qg)
        m = jnp.max(st, axis=0, keepdims=True)
        p = jnp.exp2(st - m)
        l = jnp.sum(p, axis=0, keepdims=True)
        ot = _dot(vt_ref[0, g * HEAD_DIM:(g + 1) * HEAD_DIM, :], p.astype(BF16)) / l
        for hh, h in enumerate(heads):
            ot_ref[0, h * HEAD_DIM:(h + 1) * HEAD_DIM, :] = ot[:, hh * tq:(hh + 1) * tq].astype(ot_ref.dtype)


def _attention(qt, k, vt, tq_pref=256):
    b, _, t = qt.shape
    tk = k.shape[2]
    tq = _row_tile(t, tq_pref)
    return pl.pallas_call(
        _attn_kernel,
        grid=(b, t // tq),
        in_specs=[
            pl.BlockSpec((1, ATTN_WIDTH, tq), lambda bi, i: (bi, 0, i)),
            pl.BlockSpec((1, ATTN_KV_HEADS, tk, HEAD_DIM), lambda bi, i: (bi, 0, 0, 0)),
            pl.BlockSpec((1, KV_WIDTH, tk), lambda bi, i: (bi, 0, 0)),
        ],
        out_specs=pl.BlockSpec((1, ATTN_WIDTH, tq), lambda bi, i: (bi, 0, i)),
        out_shape=jax.ShapeDtypeStruct((b, ATTN_WIDTH, t), BF16),
        compiler_params=_cparams("parallel", "parallel"),
        name="attention",
    )(qt, k, vt)


def _conformer_kernel(a_ref, g_ref, ap_ref, gp_ref, an_ref, gn_ref, dw_ref, db_ref, lg_ref, lb_ref,
                      o_ref, u_ref, sh_ref, *, tm):
    i = pl.program_id(1)
    nt = pl.num_programs(1)
    u_ref[0:HALO] = jnp.where(i == 0, 0.0, ap_ref[0] * jax.nn.sigmoid(gp_ref[0]))
    u_ref[HALO:HALO + tm] = a_ref[0] * jax.nn.sigmoid(g_ref[0])
    u_ref[HALO + tm:2 * HALO + tm] = jnp.where(i == nt - 1, 0.0, an_ref[0] * jax.nn.sigmoid(gn_ref[0]))
    span = sh_ref.shape[1]
    for p in range(V7X_SUBLANES):
        sh_ref[p] = u_ref[pl.ds(p, span), :]
    half = CONV_KERNEL // 2
    acc = jnp.zeros((tm, u_ref.shape[1]), F32) + db_ref[...]
    for j in range(CONV_KERNEL):
        start = HALO - half + j
        base = start - start % V7X_SUBLANES
        acc = acc + sh_ref[start % V7X_SUBLANES, base:base + tm, :] * dw_ref[j:j + 1, :]
    mean = jnp.mean(acc, axis=-1, keepdims=True)
    cen = acc - mean
    var = jnp.mean(cen * cen, axis=-1, keepdims=True)
    y = cen * lax.rsqrt(var + LN_EPS) * lg_ref[...] + lb_ref[...]
    o_ref[0] = _silu(y).astype(o_ref.dtype)


def _conformer_conv(proj, dw, dw_b, ln_g, ln_b, tm_pref=256):
    b, t, _ = proj.shape
    cw = dw.shape[1]
    tm = _row_tile(t, tm_pref)
    hb = tm // HALO
    nhb = t // HALO
    dwp = jnp.zeros((32, cw), F32).at[:CONV_KERNEL].set(dw)
    prev = lambda c: (lambda bi, i: (bi, jnp.maximum(i * hb - 1, 0), c))
    nxt = lambda c: (lambda bi, i: (bi, jnp.minimum((i + 1) * hb, nhb - 1), c))
    vec = pl.BlockSpec((1, cw), lambda bi, i: (0, 0))
    return pl.pallas_call(
        functools.partial(_conformer_kernel, tm=tm),
        grid=(b, t // tm),
        in_specs=[
            pl.BlockSpec((1, tm, cw), lambda bi, i: (bi, i, 1)),
            pl.BlockSpec((1, tm, cw), lambda bi, i: (bi, i, 2)),
            pl.BlockSpec((1, HALO, cw), prev(1)),
            pl.BlockSpec((1, HALO, cw), prev(2)),
            pl.BlockSpec((1, HALO, cw), nxt(1)),
            pl.BlockSpec((1, HALO, cw), nxt(2)),
            pl.BlockSpec((32, cw), lambda bi, i: (0, 0)),
            vec, vec, vec,
        ],
        out_specs=pl.BlockSpec((1, tm, cw), lambda bi, i: (bi, i, 0)),
        out_shape=jax.ShapeDtypeStruct((b, t, cw), BF16),
        scratch_shapes=[pltpu.VMEM((tm + 2 * HALO, cw), F32),
                        pltpu.VMEM((V7X_SUBLANES, tm + 2 * HALO - V7X_SUBLANES, cw), F32)],
        compiler_params=_cparams("parallel", "parallel"),
        name="conformer_conv",
    )(proj, proj, proj, proj, proj, proj, dwp, dw_b.reshape(1, cw), ln_g.reshape(1, cw), ln_b.reshape(1, cw))


def _hyb_out_kernel(x_ref, at_ref, c_ref, gt_ref, wa_ref, wc_ref, o_ref):
    y = _dot_tn(at_ref[0], wa_ref[...]) + _dot(c_ref[0], wc_ref[...])
    o_ref[0] = x_ref[0] + gt_ref[0] * y


def _hybrid_out(x, attn_t, conv, gate, w_out, tm_pref=512):
    b, t, d = x.shape
    tm = _row_tile(t, tm_pref)
    aw = attn_t.shape[1]
    cw = conv.shape[-1]
    return pl.pallas_call(
        _hyb_out_kernel,
        grid=(b, t // tm),
        in_specs=[
            pl.BlockSpec((1, tm, d), lambda bi, i: (bi, i, 0)),
            pl.BlockSpec((1, aw, tm), lambda bi, i: (bi, 0, i)),
            pl.BlockSpec((1, tm, cw), lambda bi, i: (bi, i, 0)),
            pl.BlockSpec((1, 1, d), lambda bi, i: (bi, 0, 0)),
            pl.BlockSpec((aw, d), lambda bi, i: (0, 0)),
            pl.BlockSpec((cw, d), lambda bi, i: (0, 0)),
        ],
        out_specs=pl.BlockSpec((1, tm, d), lambda bi, i: (bi, i, 0)),
        out_shape=jax.ShapeDtypeStruct((b, t, d), F32),
        compiler_params=_cparams("parallel", "parallel"),
        name="hybrid_out",
    )(x, attn_t, conv, gate, w_out[:aw], w_out[aw:])


def _rwkv_feat_kernel(x_ref, xp_ref, xn_ref, g_ref, sh_ref, sc_ref, mu_ref, wr_ref, wk_ref, wv_ref,
                      g1_ref, g2_ref, w1_ref, w2f_ref, w2r_ref, a1_ref, a2f_ref, a2r_ref,
                      w0_ref, a0_ref, kk_ref, ka_ref, u_ref, ones_ref,
                      r_o, v_o, kkn_o, gg_o, bonus_o, lwf_o, kdf_o, bf_o, lwr_o, kdr_o, br_o, *, tm):
    i = pl.program_id(1)
    nt = pl.num_programs(1)
    g, sh, sc = g_ref[...], sh_ref[0], sc_ref[0]
    h = _norm_mod(x_ref[0], g, sh, sc)
    hp = jnp.where(i == 0, 0.0, _norm_mod(xp_ref[0, HALO - 1:HALO, :], g, sh, sc))
    hn = jnp.where(i == nt - 1, 0.0, _norm_mod(xn_ref[0, 0:1, :], g, sh, sc))
    rows = lax.broadcasted_iota(jnp.int32, (tm, 1), 0)
    up = jnp.where(rows == 0, hp, pltpu.roll(h, 1, 0))
    dn = jnp.where(rows == tm - 1, hn, pltpu.roll(h, tm - 1, 0))
    xx = 0.5 * (up + dn) - h

    def mix(n):
        return (h + xx * mu_ref[n:n + 1, :]).astype(BF16)

    r = _dot(mix(0), wr_ref[...])
    k = _dot(mix(2), wk_ref[...])
    v = _dot(mix(3), wv_ref[...])
    gg = _dot(jax.nn.sigmoid(_dot(mix(5), g1_ref[...])).astype(BF16), g2_ref[...])
    tl = jnp.tanh(_dot(mix(1), w1_ref[...])).astype(BF16)
    al = _dot(mix(4), a1_ref[...]).astype(BF16)

    ones_bd = ones_ref[...]
    kkf = k * kk_ref[...]
    kkn = kkf * jnp.minimum(lax.rsqrt(_head_sum(kkf * kkf, ones_bd)), 1e12)
    r_o[0] = r.astype(r_o.dtype)
    v_o[0] = v.astype(v_o.dtype)
    kkn_o[0] = kkn.astype(kkn_o.dtype)
    gg_o[0] = gg.astype(gg_o.dtype)

    bonus = jnp.zeros_like(r)
    outs = ((w2f_ref, a2f_ref, lwf_o, kdf_o, bf_o), (w2r_ref, a2r_ref, lwr_o, kdr_o, br_o))
    for dd, (w2_ref, a2_ref, lw_o, kd_o, b_o) in enumerate(outs):
        z = -(w0_ref[dd:dd + 1, :] + _dot(tl, w2_ref[...]))
        softplus = jnp.maximum(z, 0.0) + jnp.log(1.0 + jnp.exp(-jnp.abs(z)))
        lw_o[0] = -jnp.exp(-softplus - 0.5)
        a = jax.nn.sigmoid(a0_ref[dd:dd + 1, :] + _dot(al, a2_ref[...]))
        kd = k * (1.0 + (a - 1.0) * ka_ref[...])
        kd_o[0] = kd.astype(kd_o.dtype)
        b_o[0] = (kkn * a).astype(b_o.dtype)
        bonus = bonus + _head_sum(r * kd * u_ref[dd:dd + 1, :], ones_bd) * v
    bonus_o[0] = bonus.astype(bonus_o.dtype)


def _rwkv_features(x, g, shift, scale, p, ones_bd, tm_pref=512):
    b, t, d = x.shape
    tm = _row_tile(t, tm_pref)
    hb = tm // HALO
    nhb = t // HALO
    row = pl.BlockSpec((1, tm, d), lambda bi, i: (bi, i, 0))
    vec3 = pl.BlockSpec((1, 1, d), lambda bi, i: (bi, 0, 0))

    def full(a):
        return pl.BlockSpec(a.shape, lambda bi, i: (0,) * a.ndim)

    consts = [p["mu"], p["wr"], p["wk"], p["wv"], p["g1"], p["g2"], p["w1"], p["w2f"], p["w2r"],
              p["a1"], p["a2f"], p["a2r"], p["w0"], p["a0"], p["kk"], p["ka"], p["u"], ones_bd]
    out_dtypes = [BF16] * 5 + [F32, BF16, BF16] * 2
    return pl.pallas_call(
        functools.partial(_rwkv_feat_kernel, tm=tm),
        grid=(b, t // tm),
        in_specs=[
            row,
            pl.BlockSpec((1, HALO, d), lambda bi, i: (bi, jnp.maximum(i * hb - 1, 0), 0)),
            pl.BlockSpec((1, HALO, d), lambda bi, i: (bi, jnp.minimum((i + 1) * hb, nhb - 1), 0)),
            pl.BlockSpec((1, d), lambda bi, i: (0, 0)),
            vec3, vec3,
        ] + [full(a) for a in consts],
        out_specs=[row] * len(out_dtypes),
        out_shape=[jax.ShapeDtypeStruct((b, t, d), dt) for dt in out_dtypes],
        compiler_params=_cparams("parallel", "parallel"),
        name="rwkv_features",
    )(x, x, x, g.reshape(1, d), shift, scale, *consts)


def _split_bf16(x, n):
    parts, rem = [], x
    for _ in range(n):
        p = rem.astype(BF16)
        parts.append(p)
        rem = rem - p.astype(F32)
    return parts


def _mm_parts(a_parts, b_parts, fn):
    n = max(len(a_parts), len(b_parts))
    acc = None
    for i in reversed(range(len(a_parts))):
        for j in reversed(range(len(b_parts))):
            if i + j < n:
                term = fn(a_parts[i], b_parts[j])
                acc = term if acc is None else acc + term
    return acc


def _bdot(a, b):
    return lax.dot_general(a, b, (((2,), (1,)), ((0,), (0,))), preferred_element_type=F32)


def _bdot_nt(a, b):
    return lax.dot_general(a, b, (((2,), (2,)), ((0,), (0,))), preferred_element_type=F32)


def _bdot_tn(a, b):
    return lax.dot_general(a, b, (((1,), (1,)), ((0,), (0,))), preferred_element_type=F32)


def _mm_stacked(a_parts, b_parts, fn, rows):
    if len(a_parts) != 2 or len(b_parts) != 2:
        return _mm_parts(a_parts, b_parts, fn)
    both = fn(jnp.concatenate(a_parts, axis=1), b_parts[0])
    return fn(a_parts[0], b_parts[1]) + both[:, rows:] + both[:, :rows]


def _scan_kernel(rf_ref, vf_ref, kkf_ref, rr_ref, vr_ref, kkr_ref, lwf_ref, kdf_ref, bf_ref,
                 lwr_ref, kdr_ref, br_ref, s0_ref, ms_ref, mi_ref, lvl_ref, bd_ref, eye_ref,
                 yf_ref, yr_ref, sf_ref, st_ref, *, passes):
    c = pl.program_id(1)
    nc = pl.num_programs(1)
    L = rf_ref.shape[1]
    W = V7X_MXU_DIM
    G = rf_ref.shape[2] // W

    @pl.when(c == 0)
    def _():
        st_ref[...] = s0_ref[0]

    mask_strict = ms_ref[...]
    mask_incl = mi_ref[...]
    bd = bd_ref[...][None]
    bd16 = bd.astype(BF16)
    eye = eye_ref[...]
    eye_row = jnp.concatenate([eye[:L, :L]] * SCAN_GROUP, axis=1)[None]

    def groups(a):
        return jnp.stack([a[:, g * W:(g + 1) * W] for g in range(G)], axis=0)

    def expand(parts):
        return [jnp.concatenate([p] * SCAN_GROUP, axis=1) * bd16 for p in parts]

    def cat(parts_a, parts_b):
        return [jnp.concatenate([a, b], axis=1) for a, b in zip(parts_a, parts_b)]

    def prep(r_ref, v_ref, kk_ref, lw_ref, kd_ref, b_ref, tri, last):
        lw = lw_ref[0]
        cs = _mm_parts([tri.astype(BF16)], _split_bf16(lw, passes["cs"]), _dot)
        gam = jnp.exp(cs)
        gam_inv = jnp.exp(-cs)
        gam_end = gam[last:last + 1, :]
        kt = kd_ref[0] * gam_inv
        bt = b_ref[0] * gam_inv
        return dict(kkg=groups(kk_ref[0] * jnp.exp(cs - lw)), rg=groups(r_ref[0] * gam), kt=groups(kt),
                    bt=groups(bt), kb=groups(jnp.concatenate([kt * gam_end, -(bt * gam_end)], axis=0)),
                    v=groups(v_ref[0]), gend=groups(gam_end))

    fw = prep(rf_ref, vf_ref, kkf_ref, lwf_ref, kdf_ref, bf_ref, mi_ref[0, :, :L], L - 1)
    rv = prep(rr_ref, vr_ref, kkr_ref, lwr_ref, kdr_ref, br_ref, mi_ref[G, :, :L], 0)
    both_dirs = {k: jnp.concatenate([fw[k], rv[k]], axis=0) for k in fw}
    kkg, rg, kt, bt, kb, v, gend = (both_dirs[k] for k in ("kkg", "rg", "kt", "bt", "kb", "v", "gend"))
    st = st_ref[...]

    pa, pb = passes["sc"]
    lhs = jnp.concatenate([kkg, rg], axis=1)
    lhs_p = _split_bf16(lhs, max(pa, passes["sprod"][0]))
    rhs_p = cat(expand(_split_bf16(bt, pb)), expand(_split_bf16(kt, pb)))
    sc = _mm_parts(lhs_p[:pa], rhs_p, _bdot_nt)
    m_row = sc[:, :L, :W] * mask_strict
    n_row = sc[:, :L, W:] * mask_strict
    ab_row = sc[:, L:, :W] * mask_incl
    ak_row = sc[:, L:, W:] * mask_incl

    pa, pb = passes["inv"]
    t_row = eye_row - m_row * lvl_ref[0][None]
    for lv in range(1, lvl_ref.shape[0]):
        off_p = expand(_split_bf16(m_row * lvl_ref[lv][None], pb))
        x = _mm_stacked(_split_bf16(t_row, pa), off_p, _bdot, L)
        t_row = t_row - _mm_stacked(_split_bf16(x, pa), expand(_split_bf16(t_row, pb)), _bdot, L)

    pa, pb = passes["sprod"]
    s_prod = _mm_parts(lhs_p[:pa], _split_bf16(st, pb), _bdot)
    pa, pb = passes["nv"]
    v_p = _split_bf16(v, max(pb, passes["upd"][1]))
    nv = _mm_parts(_split_bf16(jnp.concatenate([n_row, ak_row], axis=1), pa), expand(v_p[:pb]), _bdot)
    pa, pb = passes["u"]
    u = _mm_parts(_split_bf16(t_row, pa), expand(_split_bf16(s_prod[:, :L] + nv[:, :L], pb)), _bdot)
    pa, pb = passes["abu"]
    u_p = _split_bf16(u, max(pb, passes["upd"][1]))
    y = s_prod[:, L:] + nv[:, L:] - _mm_parts(_split_bf16(ab_row, pa), expand(u_p[:pb]), _bdot)
    for g in range(G):
        yf_ref[0, :, g * W:(g + 1) * W] = y[g]
        yr_ref[0, :, g * W:(g + 1) * W] = y[G + g]

    pa, pb = passes["upd"]
    upd = _mm_parts(_split_bf16(kb, pa), cat(v_p[:pb], u_p[:pb]), _bdot_tn) * bd
    gcol = jnp.sum(eye[None] * gend, axis=2, keepdims=True)
    st_ref[...] = st * gcol + upd

    @pl.when(c == nc - 1)
    def _():
        sf_ref[0] = st_ref[...]


def _scan_masks(L, groups):
    t = jnp.arange(L)[:, None]
    i = jnp.arange(L)[None, :]
    tile = lambda m: jnp.broadcast_to(jnp.tile(m.astype(F32), (1, SCAN_GROUP))[None], (groups, L, SCAN_GROUP * L))
    strict = jnp.concatenate([tile(i < t), tile(i > t)], axis=0)
    incl = jnp.concatenate([tile(i <= t), tile(i >= t)], axis=0)
    sizes = [2 ** k for k in range(L.bit_length() - 1)]
    levels = jnp.stack([jnp.tile(((t // (2 * s) == i // (2 * s)) & (t // s != i // s)).astype(F32),
                                 (1, SCAN_GROUP)) for s in sizes], axis=0)
    return strict, incl, levels


def _wkv_scan(r, v, kk, lw_f, kd_f, b_f, lw_r, kd_r, b_r, s0):
    b, t, d = r.shape
    L = SCAN_CHUNK
    nc = t // L
    W = V7X_MXU_DIM
    ng = d // W
    ms, mi, lvl = _scan_masks(L, ng)
    hid = jnp.arange(W) // HEAD_DIM
    bd = (hid[:, None] == hid[None, :]).astype(F32)
    eye = jnp.eye(W, dtype=F32)
    fwd = pl.BlockSpec((1, L, d), lambda bi, c: (bi, c, 0))
    rev = pl.BlockSpec((1, L, d), lambda bi, c: (bi, nc - 1 - c, 0))
    state = pl.BlockSpec((1, 2 * ng, W, W), lambda bi, c: (bi, 0, 0, 0))
    const = lambda a: pl.BlockSpec(a.shape, lambda bi, c: (0,) * a.ndim)
    return pl.pallas_call(
        functools.partial(_scan_kernel, passes=SCAN_PASSES),
        grid=(b, nc),
        in_specs=[fwd] * 3 + [rev] * 3 + [fwd] * 3 + [rev] * 3 + [state, const(ms), const(mi), const(lvl), const(bd), const(eye)],
        out_specs=[fwd, rev, state],
        out_shape=[jax.ShapeDtypeStruct((b, t, d), F32), jax.ShapeDtypeStruct((b, t, d), F32),
                   jax.ShapeDtypeStruct((b, 2 * ng, W, W), F32)],
        scratch_shapes=[pltpu.VMEM((2 * ng, W, W), F32)],
        compiler_params=_cparams("parallel", "arbitrary"),
        name="wkv_scan",
    )(r, v, kk, r, v, kk, lw_f, kd_f, b_f, lw_r, kd_r, b_r, s0, ms, mi, lvl, bd, eye)


def _rwkv_out_kernel(x_ref, yf_ref, yr_ref, bonus_ref, gg_ref, gt_ref, lg_ref, lb_ref, wo_ref, ones_ref, o_ref):
    ones_bd = ones_ref[...]
    inv = 1.0 / HEAD_DIM
    y = yf_ref[0] + yr_ref[0]
    cen = y - _head_sum(y, ones_bd) * inv
    var = _head_sum(cen * cen, ones_bd) * inv
    yn = cen * lax.rsqrt(var + LNX_EPS) * lg_ref[...] + lb_ref[...]
    z = ((yn + bonus_ref[0]) * gg_ref[0]).astype(BF16)
    o_ref[0] = x_ref[0] + gt_ref[0] * _dot(z, wo_ref[...])


def _rwkv_out(x, yf, yr, bonus, gg, gate, lnx_g, lnx_b, wo, ones_bd, tm_pref=256):
    b, t, d = x.shape
    tm = _row_tile(t, tm_pref)
    row = pl.BlockSpec((1, tm, d), lambda bi, i: (bi, i, 0))
    vec = pl.BlockSpec((1, d), lambda bi, i: (0, 0))
    return pl.pallas_call(
        _rwkv_out_kernel,
        grid=(b, t // tm),
        in_specs=[row] * 5 + [
            pl.BlockSpec((1, 1, d), lambda bi, i: (bi, 0, 0)), vec, vec,
            pl.BlockSpec((d, d), lambda bi, i: (0, 0)),
            pl.BlockSpec((V7X_MXU_DIM, V7X_MXU_DIM), lambda bi, i: (0, 0)),
        ],
        out_specs=row,
        out_shape=jax.ShapeDtypeStruct((b, t, d), F32),
        compiler_params=_cparams("parallel", "parallel"),
        name="rwkv_out",
    )(x, yf, yr, bonus, gg, gate, lnx_g.reshape(1, d), lnx_b.reshape(1, d), wo, ones_bd)


def _rope_tables(t):
    n_freq = HEAD_DIM // 4
    inv = ROPE_THETA ** (-jnp.arange(n_freq, dtype=F32) / n_freq)
    pos = jnp.arange(t, dtype=jnp.int32)
    row = (pos // GRID_W).astype(F32)[:, None] * inv
    col = (pos % GRID_W).astype(F32)[:, None] * inv
    cos = jnp.concatenate([jnp.cos(row)] * 2 + [jnp.cos(col)] * 2, axis=1)
    sin = jnp.concatenate([-jnp.sin(row), jnp.sin(row), -jnp.sin(col), jnp.sin(col)], axis=1)
    return jnp.tile(cos, (1, ATTN_Q_HEADS)), jnp.tile(sin, (1, ATTN_Q_HEADS))


def _hybrid_layer(xc, xl, mc, ml, norm_g, w_in, q_g, k_g, dw, dw_b, ln_g, ln_b, w_out, tables, ones_bd):
    d = xl.shape[-1]
    o1, o2, o3 = ATTN_WIDTH, ATTN_WIDTH + KV_WIDTH, ATTN_WIDTH + 2 * KV_WIDTH
    w_perm = jnp.concatenate([w_in[:, :o1], w_in[:, o3:], w_in[:, o1:o3]], axis=1).astype(BF16)
    w_out16 = w_out.astype(BF16)
    (cos_l, sin_l), (cos_c, sin_c) = tables
    pl_ = _norm_mod_matmul(xl, norm_g, ml[0], ml[1], w_perm)
    pc_ = _norm_mod_matmul(xc, norm_g, mc[0], mc[1], w_perm)
    ql, kl, vl = _qk_prep(pl_, cos_l, sin_l, q_g, k_g, ones_bd)
    qc, kc, vc = _qk_prep(pc_, cos_c, sin_c, q_g, k_g, ones_bd)
    k_all = jnp.concatenate([kc, kl], axis=2)
    v_all = jnp.concatenate([vc, vl], axis=2)
    attn_l = _attention(ql, k_all, v_all)
    attn_c = _attention(qc, kc, vc)
    conv_l = _conformer_conv(pl_, dw, dw_b, ln_g, ln_b)
    conv_c = _conformer_conv(pc_, dw, dw_b, ln_g, ln_b)
    xl = _hybrid_out(xl, attn_l, conv_l, ml[2], w_out16)
    xc = _hybrid_out(xc, attn_c, conv_c, mc[2], w_out16)
    return xc, xl


def _rwkv_layer(xc, xl, mc, ml, norm_g, mu, wr, wk, wv, wo, w0, w1, w2, a0, a1, a2, g1, g2,
                k_k, k_a, u, lnx_g, lnx_b, ones_bd, ctx_out):
    b, _, d = xl.shape
    lora_w = w1.shape[-1]
    zeros_w = jnp.zeros((lora_w, d), F32)
    p = {
        "mu": jnp.zeros((V7X_SUBLANES, d), F32).at[:6].set(mu),
        "wr": wr.astype(BF16), "wk": wk.astype(BF16), "wv": wv.astype(BF16),
        "g1": g1.astype(BF16), "g2": g2.astype(BF16),
        "w1": jnp.concatenate([w1[0], w1[1]], axis=1).astype(BF16),
        "w2f": jnp.concatenate([w2[0], zeros_w], axis=0).astype(BF16),
        "w2r": jnp.concatenate([zeros_w, w2[1]], axis=0).astype(BF16),
        "a1": jnp.concatenate([a1[0], a1[1]], axis=1).astype(BF16),
        "a2f": jnp.concatenate([a2[0], jnp.zeros_like(a2[1])], axis=0).astype(BF16),
        "a2r": jnp.concatenate([jnp.zeros_like(a2[0]), a2[1]], axis=0).astype(BF16),
        "w0": jnp.zeros((V7X_SUBLANES, d), F32).at[:2].set(w0),
        "a0": jnp.zeros((V7X_SUBLANES, d), F32).at[:2].set(a0),
        "kk": k_k.reshape(1, d), "ka": k_a.reshape(1, d),
        "u": jnp.zeros((V7X_SUBLANES, d), F32).at[:2].set(u.reshape(2, d)),
    }
    fc = _rwkv_features(xc, norm_g, mc[0], mc[1], p, ones_bd)
    fl = _rwkv_features(xl, norm_g, ml[0], ml[1], p, ones_bd)
    ng = d // V7X_MXU_DIM
    state = jnp.zeros((b, 2 * ng, V7X_MXU_DIM, V7X_MXU_DIM), F32)
    ys = {}
    for name, f in (("c", fc), ("l", fl)):
        r_, v_, kkn = f[:3]
        y_f, y_r, state = _wkv_scan(r_, v_, kkn, f[5], f[6], f[7], f[8], f[9], f[10], state)
        ys[(name, 0)], ys[(name, 1)] = y_f, y_r
    wo16 = wo.astype(BF16)
    xl = _rwkv_out(xl, ys[("l", 0)], ys[("l", 1)], fl[4], fl[3], ml[2], lnx_g, lnx_b, wo16, ones_bd)
    if ctx_out:
        xc = _rwkv_out(xc, ys[("c", 0)], ys[("c", 1)], fc[4], fc[3], mc[2], lnx_g, lnx_b, wo16, ones_bd)
    return xc, xl


def kernel(x, c, ctx, c_ctx, mod_w, mod_b, norm_mix, norm_ffn, ffn_w_in, ffn_dw, ffn_dw_b, ffn_w_out, hyb_w_in, hyb_q_norm, hyb_k_norm, hyb_dw, hyb_dw_b, hyb_ln_g, hyb_ln_b, hyb_w_out, rwkv_mu, rwkv_wr, rwkv_wk, rwkv_wv, rwkv_wo, rwkv_w0, rwkv_w1, rwkv_w2, rwkv_a0, rwkv_a1, rwkv_a2, rwkv_g1, rwkv_g2, rwkv_kk, rwkv_ka, rwkv_u, rwkv_lnx_g, rwkv_lnx_b, final_norm):
    b, t, d = x.shape
    n_ctx = ctx.shape[1]
    depth = mod_w.shape[0]
    assert d % V7X_MXU_DIM == 0 and t % SCAN_CHUNK == 0 and n_ctx % SCAN_CHUNK == 0
    assert t % GRID_W == 0 and t % HALO == 0 and n_ctx % HALO == 0

    m_rows = -(-(b + 1) // V7X_SUBLANES) * V7X_SUBLANES
    cvec = jnp.zeros((m_rows, d), F32).at[:b].set(c).at[b].set(c_ctx)
    mod = _modulation(cvec, mod_w, mod_b)
    mod = mod.reshape(depth, m_rows, 6, d)

    hid = jnp.arange(V7X_MXU_DIM) // HEAD_DIM
    ones_bd = (hid[:, None] == hid[None, :]).astype(BF16)
    cos_l, sin_l = _rope_tables(t)
    tables = ((cos_l, sin_l), (jnp.ones((n_ctx, ATTN_WIDTH), F32), jnp.zeros((n_ctx, ATTN_WIDTH), F32)))

    xl, xc = x, ctx
    for i in range(depth):
        last = i == depth - 1
        j = i // 2
        ml = [mod[i, :b, n][:, None, :] for n in range(6)]
        mc = [jnp.broadcast_to(mod[i, b, n][None, None, :], (b, 1, d)) for n in range(6)]
        if i % 2 == 0:
            xc, xl = _hybrid_layer(xc, xl, mc, ml, norm_mix[i], hyb_w_in[j], hyb_q_norm[j], hyb_k_norm[j],
                                   hyb_dw[j], hyb_dw_b[j], hyb_ln_g[j], hyb_ln_b[j], hyb_w_out[j],
                                   tables, ones_bd)
        else:
            xc, xl = _rwkv_layer(xc, xl, mc, ml, norm_mix[i], rwkv_mu[j], rwkv_wr[j], rwkv_wk[j], rwkv_wv[j],
                                 rwkv_wo[j], rwkv_w0[j], rwkv_w1[j], rwkv_w2[j], rwkv_a0[j], rwkv_a1[j],
                                 rwkv_a2[j], rwkv_g1[j], rwkv_g2[j], rwkv_kk[j], rwkv_ka[j], rwkv_u[j],
                                 rwkv_lnx_g[j], rwkv_lnx_b[j], ones_bd, not last)
        w_in16 = ffn_w_in[i].astype(BF16)
        w_out16 = ffn_w_out[i].astype(BF16)
        xl = _conv_ffn(xl, norm_ffn[i], ml[3], ml[4], ml[5], w_in16, ffn_dw[i], ffn_dw_b[i], w_out16,
                       final_g=final_norm if last else None)
        if not last:
            xc = _conv_ffn(xc, norm_ffn[i], mc[3], mc[4], mc[5], w_in16, ffn_dw[i], ffn_dw_b[i], w_out16)
    return xl
```

```python
import functools

import jax
import jax.numpy as jnp
from jax import lax
from jax.experimental import pallas as pl
from jax.experimental.pallas import tpu as pltpu

F32 = jnp.float32
BF16 = jnp.bfloat16
HIGHEST = lax.Precision.HIGHEST

HEAD_DIM = 64
ATTN_Q_HEADS = 8
ATTN_KV_HEADS = 2
ATTN_GROUP = ATTN_Q_HEADS // ATTN_KV_HEADS
ATTN_WIDTH = ATTN_Q_HEADS * HEAD_DIM
KV_WIDTH = ATTN_KV_HEADS * HEAD_DIM
CONV_KERNEL = 31
GRID_W = 64
ROPE_THETA = 10000.0
NORM_EPS = 1e-6
LN_EPS = 1e-5
LNX_EPS = 64e-5
LOG2_E = 1.4426950408889634

V7X_SUBLANES = 8
V7X_MXU_DIM = 256
V7X_SCOPED_VMEM_BYTES = 60000 * 1024
VMEM_LIMIT_BYTES = (V7X_SCOPED_VMEM_BYTES // (1024 * 1024) - 1) * 1024 * 1024

HALO = 16
SCAN_CHUNK = 64
SCAN_GROUP = V7X_MXU_DIM // HEAD_DIM


def _cparams(*sem):
    return pltpu.CompilerParams(dimension_semantics=sem, vmem_limit_bytes=VMEM_LIMIT_BYTES)


def _row_tile(t, pref):
    return pref if t % pref == 0 else t


def _dot(a, b, precision=None):
    return jnp.dot(a, b, preferred_element_type=F32, precision=precision)


def _dot_tn(a, b, precision=None):
    return lax.dot_general(a, b, (((0,), (0,)), ((), ())), preferred_element_type=F32, precision=precision)


def _silu(x):
    return x * jax.nn.sigmoid(x)


def _norm_mod(x, g, shift, scale):
    ms = jnp.mean(x * x, axis=-1, keepdims=True)
    return (x * lax.rsqrt(ms + NORM_EPS) * g) * (1.0 + scale) + shift


def _head_sum(x, ones_bd):
    w = x.shape[-1]
    hi = x.astype(BF16)
    lo = (x - hi.astype(F32)).astype(BF16)
    outs = []
    for s in range(0, w, V7X_MXU_DIM):
        e = min(s + V7X_MXU_DIM, w)
        g = ones_bd[: e - s, : e - s]
        outs.append(_dot(hi[:, s:e], g) + _dot(lo[:, s:e], g))
    return outs[0] if len(outs) == 1 else jnp.concatenate(outs, axis=-1)


def _mod_kernel(c_ref, w_ref, b_ref, o_ref):
    s = _silu(c_ref[...])
    o_ref[0] = _dot(s, w_ref[0], HIGHEST) + b_ref[0]


def _modulation(cvec, mod_w, mod_b):
    depth, d, n = mod_w.shape
    m = cvec.shape[0]
    tn = 2 * V7X_MXU_DIM
    return pl.pallas_call(
        _mod_kernel,
        grid=(depth, n // tn),
        in_specs=[
            pl.BlockSpec((m, d), lambda l, j: (0, 0)),
            pl.BlockSpec((1, d, tn), lambda l, j: (l, 0, j)),
            pl.BlockSpec((1, 1, tn), lambda l, j: (l, 0, j)),
        ],
        out_specs=pl.BlockSpec((1, m, tn), lambda l, j: (l, 0, j)),
        out_shape=jax.ShapeDtypeStruct((depth, m, n), F32),
        compiler_params=_cparams("parallel", "parallel"),
        name="modulation",
    )(cvec, mod_w, mod_b.reshape(depth, 1, n))


def _nmm_kernel(x_ref, g_ref, sh_ref, sc_ref, w_ref, o_ref):
    h = _norm_mod(x_ref[0], g_ref[...], sh_ref[0], sc_ref[0]).astype(BF16)
    o_ref[0] = _dot(h, w_ref[...]).astype(o_ref.dtype)


def _norm_mod_matmul(x, g, shift, scale, w, tm_pref=512, out_dtype=F32):
    b, t, d = x.shape
    n = w.shape[1]
    tm = _row_tile(t, tm_pref)
    return pl.pallas_call(
        _nmm_kernel,
        grid=(b, t // tm),
        in_specs=[
            pl.BlockSpec((1, tm, d), lambda bi, i: (bi, i, 0)),
            pl.BlockSpec((1, d), lambda bi, i: (0, 0)),
            pl.BlockSpec((1, 1, d), lambda bi, i: (bi, 0, 0)),
            pl.BlockSpec((1, 1, d), lambda bi, i: (bi, 0, 0)),
            pl.BlockSpec((d, n), lambda bi, i: (0, 0)),
        ],
        out_specs=pl.BlockSpec((1, tm, n), lambda bi, i: (bi, i, 0)),
        out_shape=jax.ShapeDtypeStruct((b, t, n), out_dtype),
        compiler_params=_cparams("parallel", "parallel"),
        name="norm_mod_matmul",
    )(x, g.reshape(1, d), shift, scale, w)


def _ffn_kernel(x_ref, xp_ref, xn_ref, g_ref, sh_ref, sc_ref, gt_ref, wg_ref, wv_ref,
                dw_ref, db_ref, wo_ref, fg_ref, o_ref, h_ref, gate_ref, acc_ref, *, tm, final_norm):
    i = pl.program_id(1)
    j = pl.program_id(2)
    nt = pl.num_programs(1)
    nf = pl.num_programs(2)

    @pl.when(j == 0)
    def _():
        g, sh, sc = g_ref[...], sh_ref[0], sc_ref[0]
        h_ref[0:HALO] = _norm_mod(xp_ref[0], g, sh, sc).astype(BF16)
        h_ref[HALO:HALO + tm] = _norm_mod(x_ref[0], g, sh, sc).astype(BF16)
        h_ref[HALO + tm:2 * HALO + tm] = _norm_mod(xn_ref[0], g, sh, sc).astype(BF16)
        acc_ref[...] = jnp.zeros_like(acc_ref)

    gate_ref[...] = _dot(h_ref[...], wg_ref[...])
    val = _dot(h_ref[HALO:HALO + tm], wv_ref[...])
    rows = lax.broadcasted_iota(jnp.int32, (tm, 1), 0)
    g_prev = jnp.where((rows == 0) & (i == 0), 0.0, gate_ref[pl.ds(HALO - 1, tm), :])
    g_next = jnp.where((rows == tm - 1) & (i == nt - 1), 0.0, gate_ref[pl.ds(HALO + 1, tm), :])
    conv = (g_prev * dw_ref[0:1, :] + gate_ref[pl.ds(HALO, tm), :] * dw_ref[1:2, :]
            + g_next * dw_ref[2:3, :] + db_ref[...])
    act = (_silu(conv) * val).astype(BF16)
    acc_ref[...] += _dot(act, wo_ref[...])

    @pl.when(j == nf - 1)
    def _():
        y = x_ref[0] + gt_ref[0] * acc_ref[...]
        if final_norm:
            ms = jnp.mean(y * y, axis=-1, keepdims=True)
            y = y * lax.rsqrt(ms + NORM_EPS) * fg_ref[...]
        o_ref[0] = y


def _conv_ffn(x, g, shift, scale, gate, w_in, dw, dw_b, w_out, final_g=None, tm_pref=512):
    b, t, d = x.shape
    f = w_out.shape[0]
    nf = 2
    fc = f // nf
    tm = _row_tile(t, tm_pref)
    hb = tm // HALO
    nhb = t // HALO
    dwp = jnp.zeros((V7X_SUBLANES, f), F32).at[:dw.shape[0]].set(dw)
    fg = jnp.ones((d,), F32) if final_g is None else final_g
    return pl.pallas_call(
        functools.partial(_ffn_kernel, tm=tm, final_norm=final_g is not None),
        grid=(b, t // tm, nf),
        in_specs=[
            pl.BlockSpec((1, tm, d), lambda bi, i, j: (bi, i, 0)),
            pl.BlockSpec((1, HALO, d), lambda bi, i, j: (bi, jnp.maximum(i * hb - 1, 0), 0)),
            pl.BlockSpec((1, HALO, d), lambda bi, i, j: (bi, jnp.minimum((i + 1) * hb, nhb - 1), 0)),
            pl.BlockSpec((1, d), lambda bi, i, j: (0, 0)),
            pl.BlockSpec((1, 1, d), lambda bi, i, j: (bi, 0, 0)),
            pl.BlockSpec((1, 1, d), lambda bi, i, j: (bi, 0, 0)),
            pl.BlockSpec((1, 1, d), lambda bi, i, j: (bi, 0, 0)),
            pl.BlockSpec((d, fc), lambda bi, i, j: (0, j)),
            pl.BlockSpec((d, fc), lambda bi, i, j: (0, nf + j)),
            pl.BlockSpec((V7X_SUBLANES, fc), lambda bi, i, j: (0, j)),
            pl.BlockSpec((1, fc), lambda bi, i, j: (0, j)),
            pl.BlockSpec((fc, d), lambda bi, i, j: (j, 0)),
            pl.BlockSpec((1, d), lambda bi, i, j: (0, 0)),
        ],
        out_specs=pl.BlockSpec((1, tm, d), lambda bi, i, j: (bi, i, 0)),
        out_shape=jax.ShapeDtypeStruct((b, t, d), F32),
        scratch_shapes=[
            pltpu.VMEM((tm + 2 * HALO, d), BF16),
            pltpu.VMEM((tm + 2 * HALO, fc), F32),
            pltpu.VMEM((tm, d), F32),
        ],
        compiler_params=_cparams("parallel", "parallel", "arbitrary"),
        name="conv_ffn",
    )(x, x, x, g.reshape(1, d), shift, scale, gate, w_in, w_in, dwp, dw_b.reshape(1, f), w_out, fg.reshape(1, d))


def _rope(x, cos, sin):
    w = x.shape[-1]
    pair = HEAD_DIM // 4
    lane = lax.broadcasted_iota(jnp.int32, x.shape, x.ndim - 1)
    partner = jnp.where(lane % (2 * pair) < pair, pltpu.roll(x, w - pair, x.ndim - 1), pltpu.roll(x, pair, x.ndim - 1))
    return x * cos + partner * sin


def _qk_prep_kernel(q_ref, k_ref, v_ref, cos_ref, sin_ref, qg_ref, kg_ref, ones_ref,
                    qo_ref, ko_ref, vo_ref):
    ones_bd = ones_ref[...]
    inv = 1.0 / HEAD_DIM
    q = q_ref[0]
    q = q * lax.rsqrt(_head_sum(q * q, ones_bd) * inv + NORM_EPS) * qg_ref[...]
    q = _rope(q, cos_ref[...], sin_ref[...])
    qo_ref[0] = (q * (HEAD_DIM ** -0.5 * LOG2_E)).T.astype(BF16)
    k = k_ref[0]
    k = k * lax.rsqrt(_head_sum(k * k, ones_bd) * inv + NORM_EPS) * kg_ref[...]
    k = _rope(k, cos_ref[:, :KV_WIDTH], sin_ref[:, :KV_WIDTH])
    for g in range(ATTN_KV_HEADS):
        ko_ref[0, g] = k[:, g * HEAD_DIM:(g + 1) * HEAD_DIM].astype(BF16)
    vo_ref[0] = v_ref[0].T.astype(BF16)


def _qk_prep(proj, cos, sin, q_g, k_g, ones_bd, tm_pref=512):
    b, t, _ = proj.shape
    tm = _row_tile(t, tm_pref)
    kcol = 3 * ATTN_WIDTH // KV_WIDTH
    return pl.pallas_call(
        _qk_prep_kernel,
        grid=(b, t // tm),
        in_specs=[
            pl.BlockSpec((1, tm, ATTN_WIDTH), lambda bi, i: (bi, i, 0)),
            pl.BlockSpec((1, tm, KV_WIDTH), lambda bi, i: (bi, i, kcol)),
            pl.BlockSpec((1, tm, KV_WIDTH), lambda bi, i: (bi, i, kcol + 1)),
            pl.BlockSpec((tm, ATTN_WIDTH), lambda bi, i: (i, 0)),
            pl.BlockSpec((tm, ATTN_WIDTH), lambda bi, i: (i, 0)),
            pl.BlockSpec((1, ATTN_WIDTH), lambda bi, i: (0, 0)),
            pl.BlockSpec((1, KV_WIDTH), lambda bi, i: (0, 0)),
            pl.BlockSpec((V7X_MXU_DIM, V7X_MXU_DIM), lambda bi, i: (0, 0)),
        ],
        out_specs=[
            pl.BlockSpec((1, ATTN_WIDTH, tm), lambda bi, i: (bi, 0, i)),
            pl.BlockSpec((1, ATTN_KV_HEADS, tm, HEAD_DIM), lambda bi, i: (bi, 0, i, 0)),
            pl.BlockSpec((1, KV_WIDTH, tm), lambda bi, i: (bi, 0, i)),
        ],
        out_shape=[
            jax.ShapeDtypeStruct((b, ATTN_WIDTH, t), BF16),
            jax.ShapeDtypeStruct((b, ATTN_KV_HEADS, t, HEAD_DIM), BF16),
            jax.ShapeDtypeStruct((b, KV_WIDTH, t), BF16),
        ],
        compiler_params=_cparams("parallel", "parallel"),
        name="qk_prep",
    )(proj, proj, proj, cos, sin, jnp.tile(q_g, ATTN_Q_HEADS).reshape(1, ATTN_WIDTH),
      jnp.tile(k_g, ATTN_KV_HEADS).reshape(1, KV_WIDTH), ones_bd)


def _attn_kernel(qt_ref, k_ref, vt_ref, ot_ref):
    tq = qt_ref.shape[2]
    for g in range(ATTN_KV_HEADS):
        heads = [g * ATTN_GROUP + hh for hh in range(ATTN_GROUP)]
        qg = jnp.concatenate([qt_ref[0, h * HEAD_DIM:(h + 1) * HEAD_DIM, :] for h in heads], axis=1)
        st = _dot(k_ref[0, g], qg)
        m = jnp.max(st, axis=0, keepdims=True)
        p = jnp.exp2(st - m)
        l = jnp.sum(p, axis=0, keepdims=True)
        ot = _dot(vt_ref[0, g * HEAD_DIM:(g + 1) * HEAD_DIM, :], p.astype(BF16)) / l
        for hh, h in enumerate(heads):
            ot_ref[0, h * HEAD_DIM:(h + 1) * HEAD_DIM, :] = ot[:, hh * tq:(hh + 1) * tq].astype(ot_ref.dtype)


def _attention(qt, k, vt, tq_pref=256):
    b, _, t = qt.shape
    tk = k.shape[2]
    tq = _row_tile(t, tq_pref)
    return pl.pallas_call(
        _attn_kernel,
        grid=(b, t // tq),
        in_specs=[
            pl.BlockSpec((1, ATTN_WIDTH, tq), lambda bi, i: (bi, 0, i)),
            pl.BlockSpec((1, ATTN_KV_HEADS, tk, HEAD_DIM), lambda bi, i: (bi, 0, 0, 0)),
            pl.BlockSpec((1, KV_WIDTH, tk), lambda bi, i: (bi, 0, 0)),
        ],
        out_specs=pl.BlockSpec((1, ATTN_WIDTH, tq), lambda bi, i: (bi, 0, i)),
        out_shape=jax.ShapeDtypeStruct((b, ATTN_WIDTH, t), BF16),
        compiler_params=_cparams("parallel", "parallel"),
        name="attention",
    )(qt, k, vt)


def _conformer_kernel(a_ref, g_ref, ap_ref, gp_ref, an_ref, gn_ref, dw_ref, db_ref, lg_ref, lb_ref,
                      o_ref, u_ref, sh_ref, *, tm):
    i = pl.program_id(1)
    nt = pl.num_programs(1)
    u_ref[0:HALO] = jnp.where(i == 0, 0.0, ap_ref[0] * jax.nn.sigmoid(gp_ref[0]))
    u_ref[HALO:HALO + tm] = a_ref[0] * jax.nn.sigmoid(g_ref[0])
    u_ref[HALO + tm:2 * HALO + tm] = jnp.where(i == nt - 1, 0.0, an_ref[0] * jax.nn.sigmoid(gn_ref[0]))
    span = sh_ref.shape[1]
    for p in range(V7X_SUBLANES):
        sh_ref[p] = u_ref[pl.ds(p, span), :]
    half = CONV_KERNEL // 2
    acc = jnp.zeros((tm, u_ref.shape[1]), F32) + db_ref[...]
    for j in range(CONV_KERNEL):
        start = HALO - half + j
        base = start - start % V7X_SUBLANES
        acc = acc + sh_ref[start % V7X_SUBLANES, base:base + tm, :] * dw_ref[j:j + 1, :]
    mean = jnp.mean(acc, axis=-1, keepdims=True)
    cen = acc - mean
    var = jnp.mean(cen * cen, axis=-1, keepdims=True)
    y = cen * lax.rsqrt(var + LN_EPS) * lg_ref[...] + lb_ref[...]
    o_ref[0] = _silu(y).astype(o_ref.dtype)


def _conformer_conv(proj, dw, dw_b, ln_g, ln_b, tm_pref=256):
    b, t, _ = proj.shape
    cw = dw.shape[1]
    tm = _row_tile(t, tm_pref)
    hb = tm // HALO
    nhb = t // HALO
    taps = -(-CONV_KERNEL // V7X_SUBLANES) * V7X_SUBLANES
    dwp = jnp.zeros((taps, cw), F32).at[:CONV_KERNEL].set(dw)
    prev = lambda c: (lambda bi, i: (bi, jnp.maximum(i * hb - 1, 0), c))
    nxt = lambda c: (lambda bi, i: (bi, jnp.minimum((i + 1) * hb, nhb - 1), c))
    vec = pl.BlockSpec((1, cw), lambda bi, i: (0, 0))
    return pl.pallas_call(
        functools.partial(_conformer_kernel, tm=tm),
        grid=(b, t // tm),
        in_specs=[
            pl.BlockSpec((1, tm, cw), lambda bi, i: (bi, i, 1)),
            pl.BlockSpec((1, tm, cw), lambda bi, i: (bi, i, 2)),
            pl.BlockSpec((1, HALO, cw), prev(1)),
            pl.BlockSpec((1, HALO, cw), prev(2)),
            pl.BlockSpec((1, HALO, cw), nxt(1)),
            pl.BlockSpec((1, HALO, cw), nxt(2)),
            pl.BlockSpec((taps, cw), lambda bi, i: (0, 0)),
            vec, vec, vec,
        ],
        out_specs=pl.BlockSpec((1, tm, cw), lambda bi, i: (bi, i, 0)),
        out_shape=jax.ShapeDtypeStruct((b, t, cw), BF16),
        scratch_shapes=[pltpu.VMEM((tm + 2 * HALO, cw), F32),
                        pltpu.VMEM((V7X_SUBLANES, tm + 2 * HALO - V7X_SUBLANES, cw), F32)],
        compiler_params=_cparams("parallel", "parallel"),
        name="conformer_conv",
    )(proj, proj, proj, proj, proj, proj, dwp, dw_b.reshape(1, cw), ln_g.reshape(1, cw), ln_b.reshape(1, cw))


def _hyb_out_kernel(x_ref, at_ref, c_ref, gt_ref, wa_ref, wc_ref, o_ref):
    y = _dot_tn(at_ref[0], wa_ref[...]) + _dot(c_ref[0], wc_ref[...])
    o_ref[0] = x_ref[0] + gt_ref[0] * y


def _hybrid_out(x, attn_t, conv, gate, w_out, tm_pref=512):
    b, t, d = x.shape
    tm = _row_tile(t, tm_pref)
    aw = attn_t.shape[1]
    cw = conv.shape[-1]
    return pl.pallas_call(
        _hyb_out_kernel,
        grid=(b, t // tm),
        in_specs=[
            pl.BlockSpec((1, tm, d), lambda bi, i: (bi, i, 0)),
            pl.BlockSpec((1, aw, tm), lambda bi, i: (bi, 0, i)),
            pl.BlockSpec((1, tm, cw), lambda bi, i: (bi, i, 0)),
            pl.BlockSpec((1, 1, d), lambda bi, i: (bi, 0, 0)),
            pl.BlockSpec((aw, d), lambda bi, i: (0, 0)),
            pl.BlockSpec((cw, d), lambda bi, i: (0, 0)),
        ],
        out_specs=pl.BlockSpec((1, tm, d), lambda bi, i: (bi, i, 0)),
        out_shape=jax.ShapeDtypeStruct((b, t, d), F32),
        compiler_params=_cparams("parallel", "parallel"),
        name="hybrid_out",
    )(x, attn_t, conv, gate, w_out[:aw], w_out[aw:])


def _rwkv_feat_kernel(x_ref, xp_ref, xn_ref, g_ref, sh_ref, sc_ref, mu_ref, wr_ref, wk_ref, wv_ref,
                      g1_ref, g2_ref, w1_ref, w2f_ref, w2r_ref, a1_ref, a2f_ref, a2r_ref,
                      w0_ref, a0_ref, kk_ref, ka_ref, u_ref, ones_ref,
                      r_o, v_o, kkn_o, gg_o, bonus_o, lwf_o, kdf_o, bf_o, lwr_o, kdr_o, br_o, *, tm):
    i = pl.program_id(1)
    nt = pl.num_programs(1)
    g, sh, sc = g_ref[...], sh_ref[0], sc_ref[0]
    h = _norm_mod(x_ref[0], g, sh, sc)
    hp = jnp.where(i == 0, 0.0, _norm_mod(xp_ref[0, HALO - 1:HALO, :], g, sh, sc))
    hn = jnp.where(i == nt - 1, 0.0, _norm_mod(xn_ref[0, 0:1, :], g, sh, sc))
    rows = lax.broadcasted_iota(jnp.int32, (tm, 1), 0)
    up = jnp.where(rows == 0, hp, pltpu.roll(h, 1, 0))
    dn = jnp.where(rows == tm - 1, hn, pltpu.roll(h, tm - 1, 0))
    xx = 0.5 * (up + dn) - h

    def mix(n):
        return (h + xx * mu_ref[n:n + 1, :]).astype(BF16)

    r = _dot(mix(0), wr_ref[...])
    k = _dot(mix(2), wk_ref[...])
    v = _dot(mix(3), wv_ref[...])
    gg = _dot(jax.nn.sigmoid(_dot(mix(5), g1_ref[...])).astype(BF16), g2_ref[...])
    tl = jnp.tanh(_dot(mix(1), w1_ref[...])).astype(BF16)
    al = _dot(mix(4), a1_ref[...]).astype(BF16)

    ones_bd = ones_ref[...]
    kkf = k * kk_ref[...]
    kkn = kkf * jnp.minimum(lax.rsqrt(_head_sum(kkf * kkf, ones_bd)), 1e12)
    r_o[0] = r.astype(r_o.dtype)
    v_o[0] = v.astype(v_o.dtype)
    kkn_o[0] = kkn.astype(kkn_o.dtype)
    gg_o[0] = gg.astype(gg_o.dtype)

    bonus = jnp.zeros_like(r)
    outs = ((w2f_ref, a2f_ref, lwf_o, kdf_o, bf_o), (w2r_ref, a2r_ref, lwr_o, kdr_o, br_o))
    for dd, (w2_ref, a2_ref, lw_o, kd_o, b_o) in enumerate(outs):
        z = -(w0_ref[dd:dd + 1, :] + _dot(tl, w2_ref[...]))
        softplus = jnp.maximum(z, 0.0) + jnp.log(1.0 + jnp.exp(-jnp.abs(z)))
        lw_o[0] = -jnp.exp(-softplus - 0.5)
        a = jax.nn.sigmoid(a0_ref[dd:dd + 1, :] + _dot(al, a2_ref[...]))
        kd = k * (1.0 + (a - 1.0) * ka_ref[...])
        kd_o[0] = kd.astype(kd_o.dtype)
        b_o[0] = (kkn * a).astype(b_o.dtype)
        bonus = bonus + _head_sum(r * kd * u_ref[dd:dd + 1, :], ones_bd) * v
    bonus_o[0] = bonus.astype(bonus_o.dtype)


def _rwkv_features(x, g, shift, scale, p, ones_bd, tm_pref=512):
    b, t, d = x.shape
    tm = _row_tile(t, tm_pref)
    hb = tm // HALO
    nhb = t // HALO
    row = pl.BlockSpec((1, tm, d), lambda bi, i: (bi, i, 0))
    vec3 = pl.BlockSpec((1, 1, d), lambda bi, i: (bi, 0, 0))

    def full(a):
        return pl.BlockSpec(a.shape, lambda bi, i: (0,) * a.ndim)

    consts = [p["mu"], p["wr"], p["wk"], p["wv"], p["g1"], p["g2"], p["w1"], p["w2f"], p["w2r"],
              p["a1"], p["a2f"], p["a2r"], p["w0"], p["a0"], p["kk"], p["ka"], p["u"], ones_bd]
    out_dtypes = [BF16] * 5 + [F32, BF16, BF16] * 2
    return pl.pallas_call(
        functools.partial(_rwkv_feat_kernel, tm=tm),
        grid=(b, t // tm),
        in_specs=[
            row,
            pl.BlockSpec((1, HALO, d), lambda bi, i: (bi, jnp.maximum(i * hb - 1, 0), 0)),
            pl.BlockSpec((1, HALO, d), lambda bi, i: (bi, jnp.minimum((i + 1) * hb, nhb - 1), 0)),
            pl.BlockSpec((1, d), lambda bi, i: (0, 0)),
            vec3, vec3,
        ] + [full(a) for a in consts],
        out_specs=[row] * len(out_dtypes),
        out_shape=[jax.ShapeDtypeStruct((b, t, d), dt) for dt in out_dtypes],
        compiler_params=_cparams("parallel", "parallel"),
        name="rwkv_features",
    )(x, x, x, g.reshape(1, d), shift, scale, *consts)


def _split_hi_lo(x):
    hi = x.astype(BF16)
    return hi, (x - hi.astype(F32)).astype(BF16)


def _bdot(a, b):
    return lax.dot_general(a, b, (((2,), (1,)), ((0,), (0,))), preferred_element_type=F32)


def _bdot_nt(a, b):
    return lax.dot_general(a, b, (((2,), (2,)), ((0,), (0,))), preferred_element_type=F32)


def _bdot_tn(a, b):
    return lax.dot_general(a, b, (((1,), (1,)), ((0,), (0,))), preferred_element_type=F32)


def _scan_kernel(rf_ref, vf_ref, kkf_ref, rr_ref, vr_ref, kkr_ref, lwf_ref, kdf_ref, bf_ref,
                 lwr_ref, kdr_ref, br_ref, s0_ref, ms_ref, mi_ref, lvl_ref, bd_ref, eye_ref,
                 yf_ref, yr_ref, sf_ref, st_ref):
    c = pl.program_id(1)
    nc = pl.num_programs(1)
    L = rf_ref.shape[1]
    W = V7X_MXU_DIM
    G = rf_ref.shape[2] // W

    @pl.when(c == 0)
    def _():
        st_ref[...] = s0_ref[0]

    mask_strict = ms_ref[...]
    mask_incl = mi_ref[...]
    bd = bd_ref[...][None]
    bd16 = bd.astype(BF16)
    eye = eye_ref[...]
    eye_row = jnp.concatenate([eye[:L, :L]] * SCAN_GROUP, axis=1)[None]

    def groups(a):
        return jnp.stack([a[:, g * W:(g + 1) * W] for g in range(G)], axis=0)

    def expand(a):
        return jnp.concatenate([a.astype(BF16)] * SCAN_GROUP, axis=1) * bd16

    def prep(r_ref, v_ref, kk_ref, lw_ref, kd_ref, b_ref, tri, last):
        lw = lw_ref[0]
        lw_hi, lw_lo = _split_hi_lo(lw)
        cs = _dot(tri.astype(BF16), lw_lo) + _dot(tri.astype(BF16), lw_hi)
        gam = jnp.exp(cs)
        gam_inv = jnp.exp(-cs)
        gam_end = gam[last:last + 1, :]
        kt = kd_ref[0] * gam_inv
        bt = b_ref[0] * gam_inv
        return dict(kkg=groups(kk_ref[0] * jnp.exp(cs - lw)), rg=groups(r_ref[0] * gam), kt=groups(kt),
                    bt=groups(bt), kb=groups(jnp.concatenate([kt * gam_end, -(bt * gam_end)], axis=0)),
                    v=groups(v_ref[0]), gend=groups(gam_end))

    fw = prep(rf_ref, vf_ref, kkf_ref, lwf_ref, kdf_ref, bf_ref, mi_ref[0, :, :L], L - 1)
    rv = prep(rr_ref, vr_ref, kkr_ref, lwr_ref, kdr_ref, br_ref, mi_ref[G, :, :L], 0)
    both_dirs = {k: jnp.concatenate([fw[k], rv[k]], axis=0) for k in fw}
    kkg, rg, kt, bt, kb, v, gend = (both_dirs[k] for k in ("kkg", "rg", "kt", "bt", "kb", "v", "gend"))
    st = st_ref[...]

    lhs = jnp.concatenate([kkg, rg], axis=1).astype(BF16)
    sc = _bdot_nt(lhs, jnp.concatenate([expand(bt), expand(kt)], axis=1))
    m_row = sc[:, :L, :W] * mask_strict
    n_row = sc[:, :L, W:] * mask_strict
    ab_row = sc[:, L:, :W] * mask_incl
    ak_row = sc[:, L:, W:] * mask_incl

    t_row = eye_row - m_row * lvl_ref[0][None]
    for lv in range(1, lvl_ref.shape[0]):
        x = _bdot(t_row.astype(BF16), expand(m_row * lvl_ref[lv][None]))
        t_row = t_row - _bdot(x.astype(BF16), expand(t_row))

    s_prod = _bdot(lhs, st.astype(BF16))
    v16 = v.astype(BF16)
    nv = _bdot(jnp.concatenate([n_row, ak_row], axis=1).astype(BF16), expand(v16))
    u = _bdot(t_row.astype(BF16), expand(s_prod[:, :L] + nv[:, :L]))
    u16 = u.astype(BF16)
    y = s_prod[:, L:] + nv[:, L:] - _bdot(ab_row.astype(BF16), expand(u16))
    for g in range(G):
        yf_ref[0, :, g * W:(g + 1) * W] = y[g]
        yr_ref[0, :, g * W:(g + 1) * W] = y[G + g]

    upd = _bdot_tn(kb.astype(BF16), jnp.concatenate([v16, u16], axis=1)) * bd
    gcol = jnp.sum(eye[None] * gend, axis=2, keepdims=True)
    st_ref[...] = st * gcol + upd

    @pl.when(c == nc - 1)
    def _():
        sf_ref[0] = st_ref[...]


def _scan_masks(L, groups):
    t = jnp.arange(L)[:, None]
    i = jnp.arange(L)[None, :]
    tile = lambda m: jnp.broadcast_to(jnp.tile(m.astype(F32), (1, SCAN_GROUP))[None], (groups, L, SCAN_GROUP * L))
    strict = jnp.concatenate([tile(i < t), tile(i > t)], axis=0)
    incl = jnp.concatenate([tile(i <= t), tile(i >= t)], axis=0)
    sizes = [2 ** k for k in range(L.bit_length() - 1)]
    levels = jnp.stack([jnp.tile(((t // (2 * s) == i // (2 * s)) & (t // s != i // s)).astype(F32),
                                 (1, SCAN_GROUP)) for s in sizes], axis=0)
    return strict, incl, levels


def _wkv_scan(r, v, kk, lw_f, kd_f, b_f, lw_r, kd_r, b_r, s0):
    b, t, d = r.shape
    L = SCAN_CHUNK
    nc = t // L
    W = V7X_MXU_DIM
    ng = d // W
    ms, mi, lvl = _scan_masks(L, ng)
    hid = jnp.arange(W) // HEAD_DIM
    bd = (hid[:, None] == hid[None, :]).astype(F32)
    eye = jnp.eye(W, dtype=F32)
    fwd = pl.BlockSpec((1, L, d), lambda bi, c: (bi, c, 0))
    rev = pl.BlockSpec((1, L, d), lambda bi, c: (bi, nc - 1 - c, 0))
    state = pl.BlockSpec((1, 2 * ng, W, W), lambda bi, c: (bi, 0, 0, 0))
    const = lambda a: pl.BlockSpec(a.shape, lambda bi, c: (0,) * a.ndim)
    return pl.pallas_call(
        _scan_kernel,
        grid=(b, nc),
        in_specs=[fwd] * 3 + [rev] * 3 + [fwd] * 3 + [rev] * 3 + [state, const(ms), const(mi), const(lvl), const(bd), const(eye)],
        out_specs=[fwd, rev, state],
        out_shape=[jax.ShapeDtypeStruct((b, t, d), F32), jax.ShapeDtypeStruct((b, t, d), F32),
                   jax.ShapeDtypeStruct((b, 2 * ng, W, W), F32)],
        scratch_shapes=[pltpu.VMEM((2 * ng, W, W), F32)],
        compiler_params=_cparams("parallel", "arbitrary"),
        name="wkv_scan",
    )(r, v, kk, r, v, kk, lw_f, kd_f, b_f, lw_r, kd_r, b_r, s0, ms, mi, lvl, bd, eye)


def _rwkv_out_kernel(x_ref, yf_ref, yr_ref, bonus_ref, gg_ref, gt_ref, lg_ref, lb_ref, wo_ref, ones_ref, o_ref):
    ones_bd = ones_ref[...]
    inv = 1.0 / HEAD_DIM
    y = yf_ref[0] + yr_ref[0]
    cen = y - _head_sum(y, ones_bd) * inv
    var = _head_sum(cen * cen, ones_bd) * inv
    yn = cen * lax.rsqrt(var + LNX_EPS) * lg_ref[...] + lb_ref[...]
    z = ((yn + bonus_ref[0]) * gg_ref[0]).astype(BF16)
    o_ref[0] = x_ref[0] + gt_ref[0] * _dot(z, wo_ref[...])


def _rwkv_out(x, yf, yr, bonus, gg, gate, lnx_g, lnx_b, wo, ones_bd, tm_pref=256):
    b, t, d = x.shape
    tm = _row_tile(t, tm_pref)
    row = pl.BlockSpec((1, tm, d), lambda bi, i: (bi, i, 0))
    vec = pl.BlockSpec((1, d), lambda bi, i: (0, 0))
    return pl.pallas_call(
        _rwkv_out_kernel,
        grid=(b, t // tm),
        in_specs=[row] * 5 + [
            pl.BlockSpec((1, 1, d), lambda bi, i: (bi, 0, 0)), vec, vec,
            pl.BlockSpec((d, d), lambda bi, i: (0, 0)),
            pl.BlockSpec((V7X_MXU_DIM, V7X_MXU_DIM), lambda bi, i: (0, 0)),
        ],
        out_specs=row,
        out_shape=jax.ShapeDtypeStruct((b, t, d), F32),
        compiler_params=_cparams("parallel", "parallel"),
        name="rwkv_out",
    )(x, yf, yr, bonus, gg, gate, lnx_g.reshape(1, d), lnx_b.reshape(1, d), wo, ones_bd)


def _rope_tables(t):
    n_freq = HEAD_DIM // 4
    inv = ROPE_THETA ** (-jnp.arange(n_freq, dtype=F32) / n_freq)
    pos = jnp.arange(t, dtype=jnp.int32)
    row = (pos // GRID_W).astype(F32)[:, None] * inv
    col = (pos % GRID_W).astype(F32)[:, None] * inv
    cos = jnp.concatenate([jnp.cos(row)] * 2 + [jnp.cos(col)] * 2, axis=1)
    sin = jnp.concatenate([-jnp.sin(row), jnp.sin(row), -jnp.sin(col), jnp.sin(col)], axis=1)
    return jnp.tile(cos, (1, ATTN_Q_HEADS)), jnp.tile(sin, (1, ATTN_Q_HEADS))


def _hybrid_layer(xc, xl, mc, ml, norm_g, w_in, q_g, k_g, dw, dw_b, ln_g, ln_b, w_out, tables, ones_bd):
    d = xl.shape[-1]
    o1, o2, o3 = ATTN_WIDTH, ATTN_WIDTH + KV_WIDTH, ATTN_WIDTH + 2 * KV_WIDTH
    w_perm = jnp.concatenate([w_in[:, :o1], w_in[:, o3:], w_in[:, o1:o3]], axis=1).astype(BF16)
    w_out16 = w_out.astype(BF16)
    (cos_l, sin_l), (cos_c, sin_c) = tables
    pl_ = _norm_mod_matmul(xl, norm_g, ml[0], ml[1], w_perm)
    pc_ = _norm_mod_matmul(xc, norm_g, mc[0], mc[1], w_perm)
    ql, kl, vl = _qk_prep(pl_, cos_l, sin_l, q_g, k_g, ones_bd)
    qc, kc, vc = _qk_prep(pc_, cos_c, sin_c, q_g, k_g, ones_bd)
    k_all = jnp.concatenate([kc, kl], axis=2)
    v_all = jnp.concatenate([vc, vl], axis=2)
    attn_l = _attention(ql, k_all, v_all)
    attn_c = _attention(qc, kc, vc)
    conv_l = _conformer_conv(pl_, dw, dw_b, ln_g, ln_b)
    conv_c = _conformer_conv(pc_, dw, dw_b, ln_g, ln_b)
    xl = _hybrid_out(xl, attn_l, conv_l, ml[2], w_out16)
    xc = _hybrid_out(xc, attn_c, conv_c, mc[2], w_out16)
    return xc, xl


def _rwkv_layer(xc, xl, mc, ml, norm_g, mu, wr, wk, wv, wo, w0, w1, w2, a0, a1, a2, g1, g2,
                k_k, k_a, u, lnx_g, lnx_b, ones_bd, ctx_out):
    b, _, d = xl.shape
    lora_w = w1.shape[-1]
    zeros_w = jnp.zeros((lora_w, d), F32)
    p = {
        "mu": jnp.zeros((V7X_SUBLANES, d), F32).at[:6].set(mu),
        "wr": wr.astype(BF16), "wk": wk.astype(BF16), "wv": wv.astype(BF16),
        "g1": g1.astype(BF16), "g2": g2.astype(BF16),
        "w1": jnp.concatenate([w1[0], w1[1]], axis=1).astype(BF16),
        "w2f": jnp.concatenate([w2[0], zeros_w], axis=0).astype(BF16),
        "w2r": jnp.concatenate([zeros_w, w2[1]], axis=0).astype(BF16),
        "a1": jnp.concatenate([a1[0], a1[1]], axis=1).astype(BF16),
        "a2f": jnp.concatenate([a2[0], jnp.zeros_like(a2[1])], axis=0).astype(BF16),
        "a2r": jnp.concatenate([jnp.zeros_like(a2[0]), a2[1]], axis=0).astype(BF16),
        "w0": jnp.zeros((V7X_SUBLANES, d), F32).at[:2].set(w0),
        "a0": jnp.zeros((V7X_SUBLANES, d), F32).at[:2].set(a0),
        "kk": k_k.reshape(1, d), "ka": k_a.reshape(1, d),
        "u": jnp.zeros((V7X_SUBLANES, d), F32).at[:2].set(u.reshape(2, d)),
    }
    fc = _rwkv_features(xc, norm_g, mc[0], mc[1], p, ones_bd)
    fl = _rwkv_features(xl, norm_g, ml[0], ml[1], p, ones_bd)
    ng = d // V7X_MXU_DIM
    state = jnp.zeros((b, 2 * ng, V7X_MXU_DIM, V7X_MXU_DIM), F32)
    ys = {}
    for name, f in (("c", fc), ("l", fl)):
        r_, v_, kkn = f[:3]
        y_f, y_r, state = _wkv_scan(r_, v_, kkn, f[5], f[6], f[7], f[8], f[9], f[10], state)
        ys[(name, 0)], ys[(name, 1)] = y_f, y_r
    wo16 = wo.astype(BF16)
    xl = _rwkv_out(xl, ys[("l", 0)], ys[("l", 1)], fl[4], fl[3], ml[2], lnx_g, lnx_b, wo16, ones_bd)
    if ctx_out:
        xc = _rwkv_out(xc, ys[("c", 0)], ys[("c", 1)], fc[4], fc[3], mc[2], lnx_g, lnx_b, wo16, ones_bd)
    return xc, xl


def kernel(x, c, ctx, c_ctx, mod_w, mod_b, norm_mix, norm_ffn, ffn_w_in, ffn_dw, ffn_dw_b, ffn_w_out, hyb_w_in, hyb_q_norm, hyb_k_norm, hyb_dw, hyb_dw_b, hyb_ln_g, hyb_ln_b, hyb_w_out, rwkv_mu, rwkv_wr, rwkv_wk, rwkv_wv, rwkv_wo, rwkv_w0, rwkv_w1, rwkv_w2, rwkv_a0, rwkv_a1, rwkv_a2, rwkv_g1, rwkv_g2, rwkv_kk, rwkv_ka, rwkv_u, rwkv_lnx_g, rwkv_lnx_b, final_norm):
    b, t, d = x.shape
    n_ctx = ctx.shape[1]
    depth = mod_w.shape[0]
    assert d % V7X_MXU_DIM == 0 and t % SCAN_CHUNK == 0 and n_ctx % SCAN_CHUNK == 0
    assert t % GRID_W == 0 and t % HALO == 0 and n_ctx % HALO == 0

    m_rows = -(-(b + 1) // V7X_SUBLANES) * V7X_SUBLANES
    cvec = jnp.zeros((m_rows, d), F32).at[:b].set(c).at[b].set(c_ctx)
    mod = _modulation(cvec, mod_w, mod_b)
    mod = mod.reshape(depth, m_rows, 6, d)

    hid = jnp.arange(V7X_MXU_DIM) // HEAD_DIM
    ones_bd = (hid[:, None] == hid[None, :]).astype(BF16)
    cos_l, sin_l = _rope_tables(t)
    tables = ((cos_l, sin_l), (jnp.ones((n_ctx, ATTN_WIDTH), F32), jnp.zeros((n_ctx, ATTN_WIDTH), F32)))

    xl, xc = x, ctx
    for i in range(depth):
        last = i == depth - 1
        j = i // 2
        ml = [mod[i, :b, n][:, None, :] for n in range(6)]
        mc = [jnp.broadcast_to(mod[i, b, n][None, None, :], (b, 1, d)) for n in range(6)]
        if i % 2 == 0:
            xc, xl = _hybrid_layer(xc, xl, mc, ml, norm_mix[i], hyb_w_in[j], hyb_q_norm[j], hyb_k_norm[j],
                                   hyb_dw[j], hyb_dw_b[j], hyb_ln_g[j], hyb_ln_b[j], hyb_w_out[j],
                                   tables, ones_bd)
        else:
            xc, xl = _rwkv_layer(xc, xl, mc, ml, norm_mix[i], rwkv_mu[j], rwkv_wr[j], rwkv_wk[j], rwkv_wv[j],
                                 rwkv_wo[j], rwkv_w0[j], rwkv_w1[j], rwkv_w2[j], rwkv_a0[j], rwkv_a1[j],
                                 rwkv_a2[j], rwkv_g1[j], rwkv_g2[j], rwkv_kk[j], rwkv_ka[j], rwkv_u[j],
                                 rwkv_lnx_g[j], rwkv_lnx_b[j], ones_bd, not last)
        w_in16 = ffn_w_in[i].astype(BF16)
        w_out16 = ffn_w_out[i].astype(BF16)
        xl = _conv_ffn(xl, norm_ffn[i], ml[3], ml[4], ml[5], w_in16, ffn_dw[i], ffn_dw_b[i], w_out16,
                       final_g=final_norm if last else None)
        if not last:
            xc = _conv_ffn(xc, norm_ffn[i], mc[3], mc[4], mc[5], w_in16, ffn_dw[i], ffn_dw_b[i], w_out16)
    return xl
```
